```python
import jax
import jax.numpy as jnp
from jax import lax
import numpy as np

D_MODEL = 1024
BATCH = 16
SEQ = 2048
DEPTH = 4
DEC_BATCH = 8
DEC_SEQ = 8192
PAST_LEN = 128

BRANCH_WIDTH = D_MODEL // 2
N_BRANCH = 4
SSD_HEAD_DIM = 64
SSD_HEADS = BRANCH_WIDTH // SSD_HEAD_DIM
SSD_GROUPS = 2
SSD_STATE = 64
SSD_CONV = 5
SSD_CHUNK = 128
SSD_CONV_CH = BRANCH_WIDTH + 2 * SSD_GROUPS * SSD_STATE
POOL_WINDOWS = (2, 4, 8, 16)
POOL_GROUPS = 4
POOL_GROUP_DIM = BRANCH_WIDTH // POOL_GROUPS
SGU_CHUNK = 128
SGU_GROUPS = 4
SGU_GROUP_DIM = BRANCH_WIDTH // SGU_GROUPS
GLA_HEADS = 4
GLA_KEY_DIM = 64
GLA_VAL_DIM = BRANCH_WIDTH // GLA_HEADS
GLA_GATE_RANK = 16
GLA_GATE_NORMALIZER = 16.0
GLA_CHUNK = 64
D_FF = 4 * D_MODEL
RMS_EPS = 1e-6
COL_SIZES = (BRANCH_WIDTH, SSD_CONV_CH, 2 * SSD_HEADS,
             BRANCH_WIDTH,
             BRANCH_WIDTH, BRANCH_WIDTH,
             GLA_HEADS * GLA_KEY_DIM, GLA_HEADS * GLA_KEY_DIM,
             BRANCH_WIDTH, BRANCH_WIDTH, 2 * GLA_GATE_RANK,
             N_BRANCH * D_MODEL)
N_IN = sum(COL_SIZES)

kernel_name = 'hybrid_bidir_ssd_pool_sgu_gla_encoder'


def _column_splits():
    return [int(c) for c in np.cumsum(np.array(COL_SIZES))[:-1]]


def _flip(t):
    return jnp.flip(t, axis=1)


def rms_norm(x, g):
    xf = x.astype(jnp.float32)
    y = xf * lax.rsqrt(jnp.mean(xf * xf, axis=-1, keepdims=True) + RMS_EPS)
    return (y * g.astype(jnp.float32)).astype(x.dtype)


def centred_depthwise_conv(x, w, b):
    pad = SSD_CONV // 2
    y = lax.conv_general_dilated(x, w.astype(x.dtype)[:, None, :], window_strides=(1,),
                                 padding=[(pad, pad)], dimension_numbers=('NWC', 'WIO', 'NWC'),
                                 feature_group_count=x.shape[-1])
    return y + b.astype(x.dtype)


def ssd_scan(x, dt, a, b_in, c_in):
    bsz, seq, n_heads, hd = x.shape
    n_groups = b_in.shape[2]
    hpg = n_heads // n_groups
    nc = seq // SSD_CHUNK
    T = SSD_CHUNK
    x = x.reshape(bsz, nc, T, n_groups, hpg, hd)
    dt = dt.reshape(bsz, nc, T, n_groups, hpg)
    bc = b_in.reshape(bsz, nc, T, n_groups, -1)
    cc = c_in.reshape(bsz, nc, T, n_groups, -1)
    acs = jnp.cumsum(dt * a.reshape(n_groups, hpg), axis=2)
    xdt = x * dt[..., None]
    causal = jnp.arange(T)[:, None] >= jnp.arange(T)[None, :]
    seg = acs[:, :, :, None] - acs[:, :, None, :]
    decay = jnp.exp(jnp.where(causal[:, :, None, None], seg, -jnp.inf))
    cb = jnp.einsum('bctgn,bcsgn->bctsg', cc, bc)
    y_diag = jnp.einsum('bctsgh,bcsghp->bctghp', cb[..., None] * decay, xdt)
    decay_to_end = jnp.exp(acs[:, :, -1:] - acs)
    states = jnp.einsum('bctgn,bctgh,bctghp->bcghpn', bc, decay_to_end, xdt)
    a_last = acs[:, :, -1]
    z = jnp.cumsum(a_last, axis=1)
    ex = (z - a_last)[:, :, None] - z[:, None, :]
    cmask = jnp.arange(nc)[:, None] > jnp.arange(nc)[None, :]
    w_chunk = jnp.exp(jnp.where(cmask[:, :, None, None], ex, -jnp.inf))
    s_in = jnp.einsum('bcdgh,bdghpn->bcghpn', w_chunk, states)
    y_off = jnp.einsum('bctgn,bcghpn,bctgh->bctghp', cc, s_in, jnp.exp(acs))
    return (y_diag + y_off).reshape(bsz, seq, n_heads, hd)


def ssd_mixer(z, xbc, dt_raw, conv_w, conv_b, a_log, dt_bias, d_skip, norm_g):
    bsz, seq, _ = z.shape
    xbc = jax.nn.silu(centred_depthwise_conv(xbc, conv_w, conv_b)).astype(jnp.float32)
    xs, bs, cs = jnp.split(xbc, [BRANCH_WIDTH, BRANCH_WIDTH + SSD_GROUPS * SSD_STATE], axis=-1)
    xs = xs.reshape(bsz, seq, SSD_HEADS, SSD_HEAD_DIM)
    bs = bs.reshape(bsz, seq, SSD_GROUPS, SSD_STATE)
    cs = cs.reshape(bsz, seq, SSD_GROUPS, SSD_STATE)
    dt = jax.nn.softplus(dt_raw.astype(jnp.float32).reshape(bsz, seq, 2, SSD_HEADS)
                         + dt_bias.astype(jnp.float32))
    a = -jnp.exp(a_log.astype(jnp.float32))
    y_f = ssd_scan(xs, dt[:, :, 0], a[0], bs, cs)
    y_b = _flip(ssd_scan(_flip(xs), _flip(dt[:, :, 1]), a[1], _flip(bs), _flip(cs)))
    y = y_f + y_b + xs * d_skip.astype(jnp.float32)[:, None]
    y = y.reshape(bsz, seq, BRANCH_WIDTH) * jax.nn.silu(z.astype(jnp.float32))
    y = rms_norm(y.reshape(bsz, seq, SSD_GROUPS, -1), norm_g.reshape(SSD_GROUPS, -1))
    return y.reshape(bsz, seq, BRANCH_WIDTH).astype(z.dtype)


def pool_mixer(p, pool_w, pool_scale):
    bsz, seq, _ = p.shape
    pf = p.astype(jnp.float32)
    csum = jnp.concatenate([jnp.zeros((bsz, 1, BRANCH_WIDTH), jnp.float32), jnp.cumsum(pf, axis=1)], axis=1)
    pos = jnp.arange(seq)
    outs = []
    for g, w in enumerate(POOL_WINDOWS):
        lo = jnp.clip(pos - w // 2, 0, seq - 1)
        hi = jnp.clip(pos + (w - 1 - w // 2), 0, seq - 1)
        sl = slice(g * POOL_GROUP_DIM, (g + 1) * POOL_GROUP_DIM)
        csg = csum[:, :, sl]
        win_sum = jnp.take(csg, hi + 1, axis=1) - jnp.take(csg, lo, axis=1)
        cnt = (hi - lo + 1).astype(jnp.float32)[None, :, None]
        outs.append(win_sum / cnt - pf[:, :, sl])
    pooled = jnp.stack(outs, axis=2)
    y = jnp.einsum('blgc,gcd->blgd', pooled, pool_w.astype(jnp.float32)).reshape(bsz, seq, BRANCH_WIDTH)
    return (y * pool_scale.astype(jnp.float32)).astype(p.dtype)


def sgu_mixer(u, v, norm_g, w_s, b_s):
    bsz, seq, _ = u.shape
    uf = jax.nn.gelu(u.astype(jnp.float32))
    vf = rms_norm(jax.nn.gelu(v.astype(jnp.float32)), norm_g)
    vf = vf.reshape(bsz, seq // SGU_CHUNK, SGU_CHUNK, SGU_GROUPS, SGU_GROUP_DIM)
    mixed = jnp.einsum('gts,bcsgd->bctgd', w_s.astype(jnp.float32), vf) \
        + b_s.astype(jnp.float32).T[None, None, :, :, None]
    return (uf * mixed.reshape(bsz, seq, BRANCH_WIDTH)).astype(u.dtype)


def gla_scan(q, k, v, g):
    bsz, seq, n_heads, kd = q.shape
    vd = v.shape[-1]
    nc = seq // GLA_CHUNK
    T = GLA_CHUNK
    q = q.reshape(bsz, nc, T, n_heads, kd)
    k = k.reshape(bsz, nc, T, n_heads, kd)
    v = v.reshape(bsz, nc, T, n_heads, vd)
    g = g.reshape(bsz, nc, T, n_heads, kd)
    bcum = jnp.cumsum(g, axis=2)
    b_last = bcum[:, :, -1]
    q_dec = q * jnp.exp(bcum)
    k_inv = k * jnp.exp(-bcum)
    k_end = k * jnp.exp(b_last[:, :, None] - bcum)
    causal = jnp.arange(T)[:, None] >= jnp.arange(T)[None, :]
    att = jnp.where(causal, jnp.einsum('bcthk,bcshk->bchts', q_dec, k_inv), 0.0)
    o_intra = jnp.einsum('bchts,bcshv->bcthv', att, v)
    states = jnp.einsum('bcthk,bcthv->bchkv', k_end, v)

    def step(s, inp):
        st, dec = inp
        return s * dec[..., None] + st, s

    init = jnp.zeros((bsz, n_heads, kd, vd), jnp.float32)
    _, s_in = lax.scan(step, init, (jnp.moveaxis(states, 1, 0), jnp.moveaxis(jnp.exp(b_last), 1, 0)))
    s_in = jnp.moveaxis(s_in, 0, 1)
    o_inter = jnp.einsum('bcthk,bchkv->bcthv', q_dec, s_in)
    return (o_intra + o_inter).reshape(bsz, seq, n_heads, vd)


def gla_mixer(q, k, v, r, g_lr, gate_w2, gate_b, norm_g):
    bsz, seq, _ = q.shape
    f32 = jnp.float32
    qf = q.astype(f32).reshape(bsz, seq, GLA_HEADS, GLA_KEY_DIM) * (GLA_KEY_DIM ** -0.5)
    kf = k.astype(f32).reshape(bsz, seq, GLA_HEADS, GLA_KEY_DIM)
    vf = v.astype(f32).reshape(bsz, seq, GLA_HEADS, GLA_VAL_DIM)
    lr = g_lr.astype(f32).reshape(bsz, seq, 2, GLA_GATE_RANK)
    gk = jax.nn.log_sigmoid(jnp.einsum('blzr,zrk->blzk', lr, gate_w2.astype(f32)) + gate_b.astype(f32))
    gk = (gk / GLA_GATE_NORMALIZER).reshape(bsz, seq, 2, GLA_HEADS, GLA_KEY_DIM)
    o_f = gla_scan(qf, kf, vf, gk[:, :, 0])
    o_b = _flip(gla_scan(_flip(qf), _flip(kf), _flip(vf), _flip(gk[:, :, 1])))
    o = rms_norm(o_f + o_b, norm_g.reshape(GLA_HEADS, GLA_VAL_DIM))
    o = o.reshape(bsz, seq, BRANCH_WIDTH) * jax.nn.silu(r.astype(f32))
    return o.astype(q.dtype)


def encoder_layer(x, norm_mix_pre, w_in, ssd_conv_w, ssd_conv_b, ssd_a_log, ssd_dt_bias, ssd_d, ssd_norm,
                  pool_w, pool_scale, sgu_norm, sgu_w, sgu_b, gla_gate_w2, gla_gate_b, gla_norm,
                  w_branch, w_out, norm_mix_post, norm_ffn_pre, w_ff1, w_ff2, norm_ffn_post):
    bsz, seq, _ = x.shape
    h = rms_norm(x, norm_mix_pre)
    proj = jnp.einsum('bld,dn->bln', h, w_in.astype(h.dtype))
    (ssd_z, ssd_xbc, ssd_dt, pool_in, sgu_u, sgu_v,
     gla_q, gla_k, gla_v, gla_r, gla_glr, gate) = jnp.split(proj, _column_splits(), axis=-1)
    branches = (
        ssd_mixer(ssd_z, ssd_xbc, ssd_dt, ssd_conv_w, ssd_conv_b, ssd_a_log, ssd_dt_bias, ssd_d, ssd_norm),
        pool_mixer(pool_in, pool_w, pool_scale),
        sgu_mixer(sgu_u, sgu_v, sgu_norm, sgu_w, sgu_b),
        gla_mixer(gla_q, gla_k, gla_v, gla_r, gla_glr, gla_gate_w2, gla_gate_b, gla_norm),
    )
    gate = jax.nn.sigmoid(gate.astype(jnp.float32).reshape(bsz, seq, N_BRANCH, D_MODEL)).astype(x.dtype)
    merged = jnp.zeros_like(x)
    for n in range(N_BRANCH):
        merged = merged + gate[:, :, n] * jnp.einsum('blw,wd->bld', branches[n], w_branch[n].astype(x.dtype))
    mix_out = jnp.einsum('bld,de->ble', merged, w_out.astype(x.dtype))
    x = x + rms_norm(mix_out, norm_mix_post)
    h2 = rms_norm(x, norm_ffn_pre)
    hidden = jnp.square(jax.nn.relu(jnp.einsum('bld,df->blf', h2, w_ff1.astype(x.dtype))))
    ff_out = jnp.einsum('blf,fd->bld', hidden, w_ff2.astype(x.dtype))
    return x + rms_norm(ff_out, norm_ffn_post)


def trunk(x, params):
    for layer in range(DEPTH):
        x = encoder_layer(x, *[p[layer] for p in params])
    return x


def setup_inputs(seed: int = 0) -> dict:
    key = jax.random.key(seed)
    ks = jax.random.split(key, 32)
    nrm = jax.random.normal
    L = DEPTH

    def gain(k, n):
        return 1.0 + 0.05 * nrm(k, (L, n), jnp.float32)

    dt0 = jnp.exp(jax.random.uniform(ks[8], (L, 2, SSD_HEADS), jnp.float32, np.log(1e-3), np.log(1e-1)))
    return {
        'x_prompt': nrm(ks[0], (BATCH, SEQ, D_MODEL), jnp.float32),
        'x_sample': nrm(ks[1], (DEC_BATCH, DEC_SEQ, D_MODEL), jnp.float32),
        'norm_mix_pre': gain(ks[2], D_MODEL),
        'w_in': nrm(ks[3], (L, D_MODEL, N_IN), jnp.float32) * D_MODEL ** -0.5,
        'ssd_conv_w': nrm(ks[4], (L, SSD_CONV, SSD_CONV_CH), jnp.float32) * SSD_CONV ** -0.5,
        'ssd_conv_b': 0.02 * nrm(ks[5], (L, SSD_CONV_CH), jnp.float32),
        'ssd_a_log': jnp.log(jax.random.uniform(ks[6], (L, 2, SSD_HEADS), jnp.float32, 1.0, 16.0)),
        'ssd_dt_bias': dt0 + jnp.log(-jnp.expm1(-dt0)),
        'ssd_d': 1.0 + 0.1 * nrm(ks[7], (L, SSD_HEADS), jnp.float32),
        'ssd_norm': gain(ks[9], BRANCH_WIDTH),
        'pool_w': nrm(ks[10], (L, POOL_GROUPS, POOL_GROUP_DIM, POOL_GROUP_DIM), jnp.float32) * POOL_GROUP_DIM ** -0.5,
        'pool_scale': gain(ks[11], BRANCH_WIDTH),
        'sgu_norm': gain(ks[12], BRANCH_WIDTH),
        'sgu_w': nrm(ks[13], (L, SGU_GROUPS, SGU_CHUNK, SGU_CHUNK), jnp.float32) * SGU_CHUNK ** -0.5,
        'sgu_b': 1.0 + 0.01 * nrm(ks[14], (L, SGU_GROUPS, SGU_CHUNK), jnp.float32),
        'gla_gate_w2': nrm(ks[15], (L, 2, GLA_GATE_RANK, GLA_HEADS * GLA_KEY_DIM), jnp.float32) * GLA_GATE_RANK ** -0.5,
        'gla_gate_b': 0.02 * nrm(ks[16], (L, 2, GLA_HEADS * GLA_KEY_DIM), jnp.float32),
        'gla_norm': gain(ks[17], BRANCH_WIDTH),
        'w_branch': nrm(ks[18], (L, N_BRANCH, BRANCH_WIDTH, D_MODEL), jnp.float32) * BRANCH_WIDTH ** -0.5,
        'w_out': nrm(ks[19], (L, D_MODEL, D_MODEL), jnp.float32) * D_MODEL ** -0.5,
        'norm_mix_post': gain(ks[20], D_MODEL),
        'norm_ffn_pre': gain(ks[21], D_MODEL),
        'w_ff1': nrm(ks[22], (L, D_MODEL, D_FF), jnp.float32) * D_MODEL ** -0.5,
        'w_ff2': nrm(ks[23], (L, D_FF, D_MODEL), jnp.float32) * D_FF ** -0.5,
        'norm_ffn_post': gain(ks[24], D_MODEL),
    }


def reference(x_prompt, x_sample, norm_mix_pre, w_in, ssd_conv_w, ssd_conv_b, ssd_a_log, ssd_dt_bias, ssd_d,
              ssd_norm, pool_w, pool_scale, sgu_norm, sgu_w, sgu_b, gla_gate_w2, gla_gate_b, gla_norm,
              w_branch, w_out, norm_mix_post, norm_ffn_pre, w_ff1, w_ff2, norm_ffn_post):
    params = (norm_mix_pre, w_in, ssd_conv_w, ssd_conv_b, ssd_a_log, ssd_dt_bias, ssd_d, ssd_norm,
              pool_w, pool_scale, sgu_norm, sgu_w, sgu_b, gla_gate_w2, gla_gate_b, gla_norm,
              w_branch, w_out, norm_mix_post, norm_ffn_pre, w_ff1, w_ff2, norm_ffn_post)
    y_prompt = trunk(x_prompt, params)
    y_sample = trunk(x_sample, params)
    return (y_prompt, y_sample)
```

```python
import functools

import jax
import jax.numpy as jnp
import numpy as np
from jax import lax
from jax.experimental import pallas as pl
from jax.experimental.pallas import tpu as pltpu

F32 = jnp.float32
BF16 = jnp.bfloat16

D_MODEL = 1024
DEPTH = 4
BRANCH_WIDTH = 512
N_BRANCH = 4
SSD_HEADS = 8
SSD_HEAD_DIM = 64
SSD_GROUPS = 2
SSD_STATE = 64
SSD_CONV = 5
SSD_CHUNK = 128
SSD_CONV_CH = 768
POOL_WINDOWS = (2, 4, 8, 16)
POOL_GROUP_DIM = 128
SGU_CHUNK = 128
SGU_GROUPS = 4
GLA_HEADS = 4
GLA_KEY_DIM = 64
GLA_VAL_DIM = 128
GLA_GATE_RANK = 16
GLA_GATE_NORMALIZER = 16.0
GLA_CHUNK = 64
GLA_QK = GLA_HEADS * GLA_KEY_DIM
D_FF = 4096
RMS_EPS = 1e-6

LANE = 128
HALO = 16
VMEM_LIMIT = 56 * 1024 * 1024

PROJ_GROUPS = (
    ("z", 512, 512, BF16), ("xbc", 768, 768, BF16), ("dt", 16, LANE, F32), ("pool", 512, 512, BF16),
    ("u", 512, 512, BF16), ("v", 512, 512, BF16), ("q", 256, 256, BF16), ("k", 256, 256, BF16),
    ("gv", 512, 512, BF16), ("r", 512, 512, BF16), ("glr", 32, LANE, F32), ("gate", 4096, 4096, BF16),
)
N_IN_PAD = sum(g[2] for g in PROJ_GROUPS)


def _sigmoid(x):
    return 1.0 / (1.0 + jnp.exp(-x))


def _softplus(x):
    return jnp.maximum(x, 0.0) + jnp.log1p(jnp.exp(-jnp.abs(x)))


def _gelu_tanh(x):
    return 0.5 * x * (1.0 + jnp.tanh(np.sqrt(2.0 / np.pi).astype(np.float32) * (x + 0.044715 * (x * x * x))))


def _rms(x, g):
    return x * lax.rsqrt(jnp.mean(x * x, axis=-1, keepdims=True) + RMS_EPS) * g


def _dot(a, b):
    return jnp.dot(a, b, preferred_element_type=F32)


def _dot_tn(a, b):
    return lax.dot_general(a, b, (((0,), (0,)), ((), ())), preferred_element_type=F32)


def _dot_nt(a, b):
    return lax.dot_general(a, b, (((1,), (1,)), ((), ())), preferred_element_type=F32)


def _split_bf16(x):
    hi = x.astype(BF16)
    lo = (x - hi.astype(F32)).astype(BF16)
    return hi, lo


def _dot_split_rhs(a_bf16, x):
    hi, lo = _split_bf16(x)
    return _dot(a_bf16, hi) + _dot(a_bf16, lo)


def _dot_split_lhs(x, b_bf16):
    hi, lo = _split_bf16(x)
    return _dot(hi, b_bf16) + _dot(lo, b_bf16)


def _wspec(arr, layer):
    nd = arr.ndim - 1
    return pl.BlockSpec((None,) + tuple(arr.shape[1:]), lambda *_: (layer,) + (0,) * nd,
                        pipeline_mode=pl.Buffered(1))


def _cspec(arr):
    nd = arr.ndim
    return pl.BlockSpec(tuple(arr.shape), lambda *_: (0,) * nd, pipeline_mode=pl.Buffered(1))


def _params(n_grid):
    return pltpu.CompilerParams(dimension_semantics=("arbitrary",) * n_grid, vmem_limit_bytes=VMEM_LIMIT)


def _in_proj_kernel(x_ref, g_ref, w_ref, *out_refs):
    h = _rms(x_ref[...], g_ref[...]).astype(BF16)
    off = 0
    for o_ref, (name, _, width, dtype) in zip(out_refs, PROJ_GROUPS):
        for c in range(0, width, 512):
            cw = min(512, width - c)
            acc = _dot(h, w_ref[:, off + c:off + c + cw])
            if name == "gate":
                acc = _sigmoid(acc)
            o_ref[:, c:c + cw] = acc.astype(dtype)
        off += width


def _in_proj(x2d, norm_g, w_in_p, layer, tm):
    n = x2d.shape[0]
    out_shape = [jax.ShapeDtypeStruct((n, g[2]), g[3]) for g in PROJ_GROUPS]
    out_specs = [pl.BlockSpec((tm, g[2]), lambda i: (i, 0)) for g in PROJ_GROUPS]
    return pl.pallas_call(
        _in_proj_kernel,
        grid=(n // tm,),
        in_specs=[pl.BlockSpec((tm, D_MODEL), lambda i: (i, 0)), _wspec(norm_g, layer), _wspec(w_in_p, layer)],
        out_specs=out_specs,
        out_shape=out_shape,
        compiler_params=_params(1),
        name="in_proj",
    )(x2d, norm_g, w_in_p)


def _tile_spec(tl, width):
    return pl.BlockSpec((None, tl, width), lambda b, i: (b, i, 0))


def _rev_tile_spec(tl, width, n_tiles):
    return pl.BlockSpec((None, tl, width), lambda b, i: (b, n_tiles - 1 - i, 0))


def _halo_specs(tl, width, n_tiles, reverse=False):
    per = tl // HALO
    last = n_tiles * per - 1

    def tile(i):
        return n_tiles - 1 - i if reverse else i

    prev = pl.BlockSpec((None, HALO, width), lambda b, i: (b, jnp.maximum(tile(i) * per - 1, 0), 0))
    nxt = pl.BlockSpec((None, HALO, width), lambda b, i: (b, jnp.minimum((tile(i) + 1) * per, last), 0))
    return prev, nxt


def _fill_halo_buffer(ext_ref, cur_ref, prev_ref, next_ref, is_first, is_last):
    tl = cur_ref.shape[0]
    ext_ref[0:HALO, :] = jnp.where(is_first, 0.0, prev_ref[...].astype(F32))
    ext_ref[HALO:HALO + tl, :] = cur_ref[...].astype(F32)
    ext_ref[HALO + tl:2 * HALO + tl, :] = jnp.where(is_last, 0.0, next_ref[...].astype(F32))


def _ssd_conv(xe_ref, xc_ref, cw_ref, cb_ref):
    tl = xc_ref.shape[0]
    pad = SSD_CONV // 2
    for r in range(0, tl, SSD_CHUNK):
        for c in range(0, SSD_CONV_CH, 256):
            acc = jnp.broadcast_to(cb_ref[:, c:c + 256], (SSD_CHUNK, 256))
            for j in range(SSD_CONV):
                start = HALO + r + j - pad
                acc = acc + cw_ref[j:j + 1, c:c + 256] * xe_ref[start:start + SSD_CHUNK, c:c + 256]
            xc_ref[r:r + SSD_CHUNK, c:c + 256] = acc * _sigmoid(acc)


def _ssd_decay_terms(dt_raw, dtb_ref, alog_ref, tril_ref, triu_ref):
    t = dt_raw.shape[0]
    lane = lax.broadcasted_iota(jnp.int32, (t, LANE), 1)
    used = lane < 2 * SSD_HEADS
    dt = jnp.where(used, _softplus(dt_raw + dtb_ref[...]), 0.0)
    a = jnp.where(used[0:1], -jnp.exp(alog_ref[...]), 0.0)
    dta = dt * a
    cf = _dot_split_rhs(tril_ref[...], dta)
    cr = _dot_split_rhs(triu_ref[...], dta)
    fwd = lane < SSD_HEADS
    acs = jnp.where(fwd, cf, cr)
    tot = jnp.where(fwd[0:1], cf[t - 1:t, :], cr[0:1, :])
    return dt, acs, tot


def _ssd_state_update(s_ref, xs, bs, w, tot, e_ref, bd_ref):
    xw = (xs * _dot(w.astype(BF16), e_ref[...])).astype(BF16)
    states = _dot_tn(bs.astype(BF16), xw) * bd_ref[...]
    dec = _dot_split_lhs(jnp.broadcast_to(jnp.exp(tot), (8, LANE)), e_ref[...])[0:1, :]
    s_ref[...] = s_ref[...] * dec + states


def _ssd_bwd_kernel(xbc_ref, xprev_ref, xnext_ref, dt_ref, cw_ref, cb_ref, alog_ref, dtb_ref,
                    tril_ref, triu_ref, eb_ref, bd_ref, sb_out_ref, xe_ref, xc_ref, s_ref):
    i = pl.program_id(1)
    n = pl.num_programs(1)
    tl = xbc_ref.shape[0]
    nci = tl // SSD_CHUNK

    @pl.when(i == 0)
    def _():
        s_ref[...] = jnp.zeros_like(s_ref)

    _fill_halo_buffer(xe_ref, xbc_ref, xprev_ref, xnext_ref, i == n - 1, i == 0)
    _ssd_conv(xe_ref, xc_ref, cw_ref, cb_ref)

    def chunk(k, carry):
        c = nci - 1 - k
        r0 = pl.multiple_of(c * SSD_CHUNK, SSD_CHUNK)
        sb_out_ref[c] = s_ref[...].astype(BF16)
        xc = xc_ref[pl.ds(r0, SSD_CHUNK), :]
        dt, acs, tot = _ssd_decay_terms(dt_ref[pl.ds(r0, SSD_CHUNK), :], dtb_ref, alog_ref, tril_ref, triu_ref)
        w = jnp.exp(tot - acs) * dt
        _ssd_state_update(s_ref, xc[:, :BRANCH_WIDTH], xc[:, BRANCH_WIDTH:BRANCH_WIDTH + LANE], w, tot,
                          eb_ref, bd_ref)
        return carry

    lax.fori_loop(0, nci, chunk, 0)


def _ssd_fwd_kernel(z_ref, xbc_ref, xprev_ref, xnext_ref, dt_ref, sb_ref, cw_ref, cb_ref, alog_ref, dtb_ref,
                    dexp_ref, ng_ref, tril_ref, triu_ref, ef_ref, eb_ref, bd_ref, o_ref, xe_ref, xc_ref, s_ref):
    i = pl.program_id(1)
    n = pl.num_programs(1)
    tl = xbc_ref.shape[0]
    nci = tl // SSD_CHUNK
    t = SSD_CHUNK

    @pl.when(i == 0)
    def _():
        s_ref[...] = jnp.zeros_like(s_ref)

    _fill_halo_buffer(xe_ref, xbc_ref, xprev_ref, xnext_ref, i == 0, i == n - 1)
    _ssd_conv(xe_ref, xc_ref, cw_ref, cb_ref)

    def chunk(c, carry):
        r0 = pl.multiple_of(c * t, t)
        xc = xc_ref[pl.ds(r0, t), :]
        xs = xc[:, :BRANCH_WIDTH]
        bs = xc[:, BRANCH_WIDTH:BRANCH_WIDTH + LANE].astype(BF16)
        cs = xc[:, BRANCH_WIDTH + LANE:]
        dt, acs, tot = _ssd_decay_terms(dt_ref[pl.ds(r0, t), :], dtb_ref, alog_ref, tril_ref, triu_ref)

        lane = lax.broadcasted_iota(jnp.int32, (t, LANE), 1)
        low = lane < SSD_STATE
        cs_b = cs.astype(BF16)
        cb = (_dot_nt(jnp.where(low, cs, 0.0).astype(BF16), bs), _dot_nt(jnp.where(low, 0.0, cs).astype(BF16), bs))

        row = lax.broadcasted_iota(jnp.int32, (t, t), 0)
        col = lax.broadcasted_iota(jnp.int32, (t, t), 1)
        tge = row >= col
        tle = row <= col
        acs_t = acs.T
        dt_t = dt.T
        hpg = SSD_HEADS // SSD_GROUPS
        y_pairs = []
        for pair in range(SSD_HEADS // 2):
            x_pair = xs[:, pair * LANE:(pair + 1) * LANE]
            acc = None
            for sub in range(2):
                h = 2 * pair + sub
                hb = SSD_HEADS + h
                arg = jnp.where(tge, acs[:, h:h + 1] - acs_t[h:h + 1, :], acs[:, hb:hb + 1] - acs_t[hb:hb + 1, :])
                wgt = jnp.where(tge, dt_t[h:h + 1, :], 0.0) + jnp.where(tle, dt_t[hb:hb + 1, :], 0.0)
                m = (cb[h // hpg] * jnp.exp(arg) * wgt).astype(BF16)
                x_half = jnp.where(low if sub == 0 else jnp.logical_not(low), x_pair, 0.0).astype(BF16)
                part = _dot(m, x_half)
                acc = part if acc is None else acc + part
            y_pairs.append(acc)
        y = jnp.concatenate(y_pairs, axis=1)

        e = jnp.exp(acs).astype(BF16)
        y = y + _dot(e, ef_ref[...]) * _dot(cs_b, s_ref[...].astype(BF16))
        y = y + _dot(e, eb_ref[...]) * _dot(cs_b, sb_ref[c])
        y = y + xs * dexp_ref[...]
        zf = z_ref[pl.ds(r0, t), :].astype(F32)
        y = y * (zf * _sigmoid(zf))
        half = BRANCH_WIDTH // SSD_GROUPS
        ng = ng_ref[...]
        y = jnp.concatenate([_rms(y[:, g * half:(g + 1) * half], ng[:, g * half:(g + 1) * half])
                             for g in range(SSD_GROUPS)], axis=1)
        o_ref[pl.ds(r0, t), :] = y.astype(o_ref.dtype)

        w = jnp.exp(tot - acs) * dt
        _ssd_state_update(s_ref, xs, bs, w, tot, ef_ref, bd_ref)
        return carry

    lax.fori_loop(0, nci, chunk, 0)


def _ssd_constants():
    t = SSD_CHUNK
    r = np.arange(t)
    tril = (r[:, None] >= r[None, :]).astype(np.float32)
    triu = (r[:, None] <= r[None, :]).astype(np.float32)
    col = np.arange(BRANCH_WIDTH)
    ef = (r[:, None] == (col[None, :] // SSD_HEAD_DIM)).astype(np.float32)
    eb = (r[:, None] == (col[None, :] // SSD_HEAD_DIM) + SSD_HEADS).astype(np.float32)
    bd = ((r[:, None] // SSD_STATE) == (col[None, :] // (BRANCH_WIDTH // SSD_GROUPS))).astype(np.float32)
    return (jnp.asarray(tril, BF16), jnp.asarray(triu, BF16), jnp.asarray(ef, BF16), jnp.asarray(eb, BF16),
            jnp.asarray(bd, F32))


def _ssd_mixer(z, xbc, dtp, p, layer, tl):
    b, l, _ = z.shape
    n_tiles = l // tl
    nci = tl // SSD_CHUNK
    tril, triu, ef, eb, bd = _ssd_constants()
    prev_r, next_r = _halo_specs(tl, SSD_CONV_CH, n_tiles, reverse=True)
    scratch = [pltpu.VMEM((tl + 2 * HALO, SSD_CONV_CH), F32), pltpu.VMEM((tl, SSD_CONV_CH), F32),
               pltpu.VMEM((LANE, BRANCH_WIDTH), F32)]
    sb = pl.pallas_call(
        _ssd_bwd_kernel,
        grid=(b, n_tiles),
        in_specs=[_rev_tile_spec(tl, SSD_CONV_CH, n_tiles), prev_r, next_r, _rev_tile_spec(tl, LANE, n_tiles),
                  _wspec(p["ssd_conv_w"], layer), _wspec(p["ssd_conv_b"], layer), _wspec(p["ssd_a_log"], layer),
                  _wspec(p["ssd_dt_bias"], layer), _cspec(tril), _cspec(triu), _cspec(eb), _cspec(bd)],
        out_specs=pl.BlockSpec((None, nci, LANE, BRANCH_WIDTH), lambda bb, i: (bb, n_tiles - 1 - i, 0, 0)),
        out_shape=jax.ShapeDtypeStruct((b, l // SSD_CHUNK, LANE, BRANCH_WIDTH), BF16),
        scratch_shapes=scratch,
        compiler_params=_params(2),
        name="ssd_bwd",
    )(xbc, xbc, xbc, dtp, p["ssd_conv_w"], p["ssd_conv_b"], p["ssd_a_log"], p["ssd_dt_bias"], tril, triu, eb, bd)

    prev_f, next_f = _halo_specs(tl, SSD_CONV_CH, n_tiles)
    return pl.pallas_call(
        _ssd_fwd_kernel,
        grid=(b, n_tiles),
        in_specs=[_tile_spec(tl, BRANCH_WIDTH), _tile_spec(tl, SSD_CONV_CH), prev_f, next_f, _tile_spec(tl, LANE),
                  pl.BlockSpec((None, nci, LANE, BRANCH_WIDTH), lambda bb, i: (bb, i, 0, 0)),
                  _wspec(p["ssd_conv_w"], layer), _wspec(p["ssd_conv_b"], layer), _wspec(p["ssd_a_log"], layer),
                  _wspec(p["ssd_dt_bias"], layer), _wspec(p["ssd_d"], layer), _wspec(p["ssd_norm"], layer),
                  _cspec(tril), _cspec(triu), _cspec(ef), _cspec(eb), _cspec(bd)],
        out_specs=_tile_spec(tl, BRANCH_WIDTH),
        out_shape=jax.ShapeDtypeStruct((b, l, BRANCH_WIDTH), BF16),
        scratch_shapes=scratch,
        compiler_params=_params(2),
        name="ssd_fwd",
    )(z, xbc, xbc, xbc, dtp, sb, p["ssd_conv_w"], p["ssd_conv_b"], p["ssd_a_log"], p["ssd_dt_bias"],
      p["ssd_d"], p["ssd_norm"], tril, triu, ef, eb, bd)


def _pool_kernel(p_ref, pprev_ref, pnext_ref, w_ref, scale_ref, o_ref, pe_ref, *, seq_len):
    i = pl.program_id(1)
    n = pl.num_programs(1)
    tl = p_ref.shape[0]
    _fill_halo_buffer(pe_ref, p_ref, pprev_ref, pnext_ref, i == 0, i == n - 1)
    pos = i * tl + lax.broadcasted_iota(jnp.int32, (tl, 1), 0)
    for g, win in enumerate(POOL_WINDOWS):
        cols = slice(g * POOL_GROUP_DIM, (g + 1) * POOL_GROUP_DIM)
        left = win // 2
        right = win - 1 - left
        acc = None
        for off in range(-left, right + 1):
            term = pe_ref[HALO + off:HALO + off + tl, cols]
            acc = term if acc is None else acc + term
        lo = jnp.maximum(pos - left, 0)
        hi = jnp.minimum(pos + right, seq_len - 1)
        cnt = (hi - lo + 1).astype(F32)
        pooled = acc / cnt - pe_ref[HALO:HALO + tl, cols]
        y = _dot(pooled.astype(BF16), w_ref[g])
        o_ref[:, cols] = (y * scale_ref[:, cols]).astype(o_ref.dtype)


def _pool_mixer(pin, p, layer, tl):
    b, l, _ = pin.shape
    n_tiles = l // tl
    prev, nxt = _halo_specs(tl, BRANCH_WIDTH, n_tiles)
    return pl.pallas_call(
        functools.partial(_pool_kernel, seq_len=l),
        grid=(b, n_tiles),
        in_specs=[_tile_spec(tl, BRANCH_WIDTH), prev, nxt, _wspec(p["pool_w"], layer), _wspec(p["pool_scale"], layer)],
        out_specs=_tile_spec(tl, BRANCH_WIDTH),
        out_shape=jax.ShapeDtypeStruct((b, l, BRANCH_WIDTH), BF16),
        scratch_shapes=[pltpu.VMEM((tl + 2 * HALO, BRANCH_WIDTH), F32)],
        compiler_params=_params(2),
        name="pool",
    )(pin, pin, pin, p["pool_w"], p["pool_scale"])


def _sgu_kernel(u_ref, v_ref, ng_ref, w_ref, bias_ref, o_ref):
    tl = u_ref.shape[0]
    gd = BRANCH_WIDTH // SGU_GROUPS
    for r in range(0, tl, SGU_CHUNK):
        rows = slice(r, r + SGU_CHUNK)
        vf = _rms(_gelu_tanh(v_ref[rows, :].astype(F32)), ng_ref[...]).astype(BF16)
        uf = _gelu_tanh(u_ref[rows, :].astype(F32))
        for g in range(SGU_GROUPS):
            cols = slice(g * gd, (g + 1) * gd)
            mixed = _dot(w_ref[g], vf[:, cols]) + bias_ref[:, cols]
            o_ref[rows, cols] = (uf[:, cols] * mixed).astype(o_ref.dtype)


def _sgu_mixer(u, v, p, layer, tl):
    b, l, _ = u.shape
    return pl.pallas_call(
        _sgu_kernel,
        grid=(b, l // tl),
        in_specs=[_tile_spec(tl, BRANCH_WIDTH), _tile_spec(tl, BRANCH_WIDTH), _wspec(p["sgu_norm"], layer),
                  _wspec(p["sgu_w"], layer), _wspec(p["sgu_bias"], layer)],
        out_specs=_tile_spec(tl, BRANCH_WIDTH),
        out_shape=jax.ShapeDtypeStruct((b, l, BRANCH_WIDTH), BF16),
        compiler_params=_params(2),
        name="sgu",
    )(u, v, p["sgu_norm"], p["sgu_w"], p["sgu_bias"])


def _gla_gates(glr, w2_ref, gb_ref, tril_ref, triu_ref, ones_ref):
    t = glr.shape[0]
    pre = _dot(glr.astype(BF16), w2_ref[...]) + gb_ref[...]
    gk = -_softplus(-pre) * (1.0 / GLA_GATE_NORMALIZER)
    gk_f, gk_b = gk[:, :GLA_QK], gk[:, GLA_QK:]
    bc_f = _dot_split_rhs(tril_ref[...], gk_f)
    bc_b = _dot_split_rhs(triu_ref[...], gk_b)
    hi_f, lo_f = _split_bf16(gk_f)
    hi_b, lo_b = _split_bf16(gk_b)
    ones = ones_ref[...]
    dcol_f = jnp.exp(_dot_tn(hi_f, ones) + _dot_tn(lo_f, ones))
    dcol_b = jnp.exp(_dot_tn(hi_b, ones) + _dot_tn(lo_b, ones))
    return (bc_f, bc_f[t - 1:t, :], dcol_f), (bc_b, bc_b[0:1, :], dcol_b)


def _gla_state_update(s_ref, k_end, v, dcol, bd_ref):
    states = _dot_tn(k_end.astype(BF16), v) * bd_ref[...]
    s_ref[...] = s_ref[...] * jnp.concatenate([dcol] * GLA_HEADS, axis=1) + states


def _gla_bwd_kernel(k_ref, v_ref, glr_ref, w2_ref, gb_ref, tril_ref, triu_ref, ones_ref, bd_ref, sb_out_ref, s_ref):
    i = pl.program_id(1)
    tl = k_ref.shape[0]
    nci = tl // GLA_CHUNK
    t = GLA_CHUNK

    @pl.when(i == 0)
    def _():
        s_ref[...] = jnp.zeros_like(s_ref)

    def chunk(kk, carry):
        c = nci - 1 - kk
        r0 = pl.multiple_of(c * t, t)
        sb_out_ref[c] = s_ref[...].astype(BF16)
        _, (bc_b, bl_b, dcol_b) = _gla_gates(glr_ref[pl.ds(r0, t), :], w2_ref, gb_ref, tril_ref, triu_ref, ones_ref)
        kf = k_ref[pl.ds(r0, t), :].astype(F32)
        _gla_state_update(s_ref, kf * jnp.exp(bl_b - bc_b), v_ref[pl.ds(r0, t), :], dcol_b, bd_ref)
        return carry

    lax.fori_loop(0, nci, chunk, 0)


def _gla_fwd_kernel(q_ref, k_ref, v_ref, r_ref, glr_ref, sb_ref, w2_ref, gb_ref, ng_ref, tril_ref, triu_ref,
                    ones_ref, bd_ref, o_ref, s_ref):
    i = pl.program_id(1)
    tl = k_ref.shape[0]
    nci = tl // GLA_CHUNK
    t = GLA_CHUNK

    @pl.when(i == 0)
    def _():
        s_ref[...] = jnp.zeros_like(s_ref)

    def chunk(c, carry):
        r0 = pl.multiple_of(c * t, t)
        (bc_f, bl_f, dcol_f), (bc_b, _, _) = _gla_gates(glr_ref[pl.ds(r0, t), :], w2_ref, gb_ref, tril_ref,
                                                        triu_ref, ones_ref)
        qf = q_ref[pl.ds(r0, t), :].astype(F32) * (GLA_KEY_DIM ** -0.5)
        kf = k_ref[pl.ds(r0, t), :].astype(F32)
        v = v_ref[pl.ds(r0, t), :]
        qd_f = qf * jnp.exp(bc_f)
        qd_b = qf * jnp.exp(bc_b)
        ki_f = (kf * jnp.exp(-bc_f)).astype(BF16)
        ki_b = (kf * jnp.exp(-bc_b)).astype(BF16)

        lane = lax.broadcasted_iota(jnp.int32, (t, GLA_QK), 1)

        def stack_heads(qd):
            return jnp.concatenate(
                [jnp.where((lane >= h * GLA_KEY_DIM) & (lane < (h + 1) * GLA_KEY_DIM), qd, 0.0)
                 for h in range(GLA_HEADS)], axis=0).astype(BF16)

        att_f = _dot_nt(stack_heads(qd_f), ki_f)
        att_b = _dot_nt(stack_heads(qd_b), ki_b)
        row = lax.broadcasted_iota(jnp.int32, (GLA_HEADS * t, t), 0) & (t - 1)
        col = lax.broadcasted_iota(jnp.int32, (GLA_HEADS * t, t), 1)
        att = (jnp.where(row >= col, att_f, 0.0) + jnp.where(row <= col, att_b, 0.0)).astype(BF16)
        o = jnp.concatenate([_dot(att[h * t:(h + 1) * t, :], v[:, h * GLA_VAL_DIM:(h + 1) * GLA_VAL_DIM])
                             for h in range(GLA_HEADS)], axis=1)
        o = o + _dot(qd_f.astype(BF16), s_ref[...].astype(BF16)) + _dot(qd_b.astype(BF16), sb_ref[c])

        ng = ng_ref[...]
        rf = r_ref[pl.ds(r0, t), :].astype(F32)
        o = jnp.concatenate([_rms(o[:, h * GLA_VAL_DIM:(h + 1) * GLA_VAL_DIM], ng[:, h * GLA_VAL_DIM:(h + 1) * GLA_VAL_DIM])
                             for h in range(GLA_HEADS)], axis=1)
        o_ref[pl.ds(r0, t), :] = (o * (rf * _sigmoid(rf))).astype(o_ref.dtype)

        _gla_state_update(s_ref, kf * jnp.exp(bl_f - bc_f), v, dcol_f, bd_ref)
        return carry

    lax.fori_loop(0, nci, chunk, 0)


def _gla_constants():
    t = GLA_CHUNK
    r = np.arange(t)
    tril = (r[:, None] >= r[None, :]).astype(np.float32)
    triu = (r[:, None] <= r[None, :]).astype(np.float32)
    ones = np.ones((t, LANE), np.float32)
    bd = ((np.arange(GLA_QK)[:, None] // GLA_KEY_DIM) == (np.arange(BRANCH_WIDTH)[None, :] // GLA_VAL_DIM))
    return (jnp.asarray(tril, BF16), jnp.asarray(triu, BF16), jnp.asarray(ones, BF16),
            jnp.asarray(bd.astype(np.float32), F32))


def _gla_mixer(q, k, v, r, glr, p, layer, tl):
    b, l, _ = q.shape
    n_tiles = l // tl
    nci = tl // GLA_CHUNK
    tril, triu, ones, bd = _gla_constants()
    state_block = (None, nci, GLA_QK, BRANCH_WIDTH)
    scratch = [pltpu.VMEM((GLA_QK, BRANCH_WIDTH), F32)]
    sb = pl.pallas_call(
        _gla_bwd_kernel,
        grid=(b, n_tiles),
        in_specs=[_rev_tile_spec(tl, GLA_QK, n_tiles), _rev_tile_spec(tl, BRANCH_WIDTH, n_tiles),
                  _rev_tile_spec(tl, LANE, n_tiles), _wspec(p["gla_w2"], layer), _wspec(p["gla_gate_b"], layer),
                  _cspec(tril), _cspec(triu), _cspec(ones), _cspec(bd)],
        out_specs=pl.BlockSpec(state_block, lambda bb, i: (bb, n_tiles - 1 - i, 0, 0)),
        out_shape=jax.ShapeDtypeStruct((b, l // GLA_CHUNK, GLA_QK, BRANCH_WIDTH), BF16),
        scratch_shapes=scratch,
        compiler_params=_params(2),
        name="gla_bwd",
    )(k, v, glr, p["gla_w2"], p["gla_gate_b"], tril, triu, ones, bd)

    return pl.pallas_call(
        _gla_fwd_kernel,
        grid=(b, n_tiles),
        in_specs=[_tile_spec(tl, GLA_QK), _tile_spec(tl, GLA_QK), _tile_spec(tl, BRANCH_WIDTH),
                  _tile_spec(tl, BRANCH_WIDTH), _tile_spec(tl, LANE),
                  pl.BlockSpec(state_block, lambda bb, i: (bb, i, 0, 0)),
                  _wspec(p["gla_w2"], layer), _wspec(p["gla_gate_b"], layer), _wspec(p["gla_norm"], layer),
                  _cspec(tril), _cspec(triu), _cspec(ones), _cspec(bd)],
        out_specs=_tile_spec(tl, BRANCH_WIDTH),
        out_shape=jax.ShapeDtypeStruct((b, l, BRANCH_WIDTH), BF16),
        scratch_shapes=scratch,
        compiler_params=_params(2),
        name="gla_fwd",
    )(q, k, v, r, glr, sb, p["gla_w2"], p["gla_gate_b"], p["gla_norm"], tril, triu, ones, bd)


def _merge_ffn_kernel(x_ref, b0_ref, b1_ref, b2_ref, b3_ref, gate_ref, wb_ref, wo_ref, gpost_ref, gpre_ref,
                      w1_ref, w2_ref, gffn_ref, o_ref):
    merged = None
    for n, b_ref in enumerate((b0_ref, b1_ref, b2_ref, b3_ref)):
        term = gate_ref[:, n * D_MODEL:(n + 1) * D_MODEL].astype(F32) * _dot(b_ref[...], wb_ref[n])
        merged = term if merged is None else merged + term
    x1 = x_ref[...] + _rms(_dot(merged.astype(BF16), wo_ref[...]), gpost_ref[...])
    h2 = _rms(x1, gpre_ref[...]).astype(BF16)
    ff = None
    for c in range(0, D_FF, D_MODEL):
        hid = jnp.square(jnp.maximum(_dot(h2, w1_ref[:, c:c + D_MODEL]), 0.0)).astype(BF16)
        part = _dot(hid, w2_ref[c:c + D_MODEL, :])
        ff = part if ff is None else ff + part
    o_ref[...] = x1 + _rms(ff, gffn_ref[...])


def _merge_ffn(x2d, branches, gate, p, layer, tm):
    n = x2d.shape[0]
    row = lambda w: pl.BlockSpec((tm, w), lambda i: (i, 0))
    return pl.pallas_call(
        _merge_ffn_kernel,
        grid=(n // tm,),
        in_specs=[row(D_MODEL)] + [row(BRANCH_WIDTH)] * N_BRANCH + [row(N_BRANCH * D_MODEL)] + [
            _wspec(p["w_branch"], layer), _wspec(p["w_out"], layer), _wspec(p["norm_mix_post"], layer),
            _wspec(p["norm_ffn_pre"], layer), _wspec(p["w_ff1"], layer), _wspec(p["w_ff2"], layer),
            _wspec(p["norm_ffn_post"], layer)],
        out_specs=row(D_MODEL),
        out_shape=jax.ShapeDtypeStruct((n, D_MODEL), F32),
        compiler_params=_params(1),
        name="merge_ffn",
    )(x2d, *branches, gate, p["w_branch"], p["w_out"], p["norm_mix_post"], p["norm_ffn_pre"], p["w_ff1"],
      p["w_ff2"], p["norm_ffn_post"])


def _pad_lanes(a, width):
    return jnp.pad(a, [(0, 0)] * (a.ndim - 1) + [(0, width - a.shape[-1])])


def _prepare_params(norm_mix_pre, w_in, ssd_conv_w, ssd_conv_b, ssd_a_log, ssd_dt_bias, ssd_d, ssd_norm, pool_w,
                    pool_scale, sgu_norm, sgu_w, sgu_b, gla_gate_w2, gla_gate_b, gla_norm, w_branch, w_out,
                    norm_mix_post, norm_ffn_pre, w_ff1, w_ff2, norm_ffn_post):
    depth = w_in.shape[0]
    pieces, off = [], 0
    for _, width, padded, _ in PROJ_GROUPS:
        pieces.append(_pad_lanes(w_in[:, :, off:off + width], padded))
        off += width
    row = lambda a: a.reshape(depth, 1, -1).astype(F32)
    w2 = jnp.zeros((depth, LANE, 2 * GLA_QK), F32)
    w2 = w2.at[:, :GLA_GATE_RANK, :GLA_QK].set(gla_gate_w2[:, 0])
    w2 = w2.at[:, GLA_GATE_RANK:2 * GLA_GATE_RANK, GLA_QK:].set(gla_gate_w2[:, 1])
    return {
        "norm_mix_pre": row(norm_mix_pre),
        "w_in": jnp.concatenate(pieces, axis=-1).astype(BF16),
        "ssd_conv_w": jnp.pad(ssd_conv_w.astype(F32), [(0, 0), (0, 8 - SSD_CONV), (0, 0)]),
        "ssd_conv_b": row(ssd_conv_b),
        "ssd_a_log": _pad_lanes(row(ssd_a_log), LANE),
        "ssd_dt_bias": _pad_lanes(row(ssd_dt_bias), LANE),
        "ssd_d": row(jnp.repeat(ssd_d, SSD_HEAD_DIM, axis=-1)),
        "ssd_norm": row(ssd_norm),
        "pool_w": pool_w.astype(BF16),
        "pool_scale": row(pool_scale),
        "sgu_norm": row(sgu_norm),
        "sgu_w": sgu_w.astype(BF16),
        "sgu_bias": jnp.repeat(jnp.swapaxes(sgu_b, 1, 2), BRANCH_WIDTH // SGU_GROUPS, axis=-1).astype(F32),
        "gla_w2": w2.astype(BF16),
        "gla_gate_b": row(gla_gate_b),
        "gla_norm": row(gla_norm),
        "w_branch": w_branch.astype(BF16),
        "w_out": w_out.astype(BF16),
        "norm_mix_post": row(norm_mix_post),
        "norm_ffn_pre": row(norm_ffn_pre),
        "w_ff1": w_ff1.astype(BF16),
        "w_ff2": w_ff2.astype(BF16),
        "norm_ffn_post": row(norm_ffn_post),
    }


def _tiles(seq_len):
    return min(512, seq_len), min(256, seq_len), min(256, seq_len)


def _trunk(x, p):
    b, l, d = x.shape
    tm_proj, tm_ffn, tl = _tiles(l)
    x2d = x.reshape(b * l, d)
    for layer in range(DEPTH):
        proj = dict(zip([g[0] for g in PROJ_GROUPS], _in_proj(x2d, p["norm_mix_pre"], p["w_in"], layer, tm_proj)))
        seq = lambda name: proj[name].reshape(b, l, -1)
        branches = (
            _ssd_mixer(seq("z"), seq("xbc"), seq("dt"), p, layer, tl),
            _pool_mixer(seq("pool"), p, layer, tl),
            _sgu_mixer(seq("u"), seq("v"), p, layer, tl),
            _gla_mixer(seq("q"), seq("k"), seq("gv"), seq("r"), seq("glr"), p, layer, tl),
        )
        branches = [br.reshape(b * l, BRANCH_WIDTH) for br in branches]
        x2d = _merge_ffn(x2d, branches, proj["gate"], p, layer, tm_ffn)
    return x2d.reshape(b, l, d)


def kernel(x_prompt, x_sample, norm_mix_pre, w_in, ssd_conv_w, ssd_conv_b, ssd_a_log, ssd_dt_bias, ssd_d, ssd_norm, pool_w, pool_scale, sgu_norm, sgu_w, sgu_b, gla_gate_w2, gla_gate_b, gla_norm, w_branch, w_out, norm_mix_post, norm_ffn_pre, w_ff1, w_ff2, norm_ffn_post):
    p = _prepare_params(norm_mix_pre, w_in, ssd_conv_w, ssd_conv_b, ssd_a_log, ssd_dt_bias, ssd_d, ssd_norm, pool_w,
                        pool_scale, sgu_norm, sgu_w, sgu_b, gla_gate_w2, gla_gate_b, gla_norm, w_branch, w_out,
                        norm_mix_post, norm_ffn_pre, w_ff1, w_ff2, norm_ffn_post)
    return (_trunk(x_prompt, p), _trunk(x_sample, p))
```

```python
import functools

import jax
import jax.numpy as jnp
import numpy as np
from jax import lax
from jax.experimental import pallas as pl
from jax.experimental.pallas import tpu as pltpu

F32 = jnp.float32
BF16 = jnp.bfloat16

D_MODEL = 1024
DEPTH = 4
BRANCH_WIDTH = 512
N_BRANCH = 4
SSD_HEADS = 8
SSD_HEAD_DIM = 64
SSD_GROUPS = 2
SSD_STATE = 64
SSD_CONV = 5
SSD_CHUNK = 128
SSD_CONV_CH = 768
POOL_WINDOWS = (2, 4, 8, 16)
POOL_GROUP_DIM = 128
POOL_CHUNK = 128
SGU_CHUNK = 128
SGU_GROUPS = 4
GLA_HEADS = 4
GLA_KEY_DIM = 64
GLA_VAL_DIM = 128
GLA_GATE_RANK = 16
GLA_GATE_NORMALIZER = 16.0
GLA_CHUNK = 64
GLA_QK = GLA_HEADS * GLA_KEY_DIM
GLA_TILE = 256
GLA_TILE_CHUNKS = GLA_TILE // GLA_CHUNK
D_FF = 4096
RMS_EPS = 1e-6

LANE = 128
HALO = 16
VMEM_LIMIT = 56 * 1024 * 1024

PROJ_GROUPS = (
    ("z", 512, 512, BF16), ("xbc", 768, 768, BF16), ("dt", 16, LANE, F32), ("pool", 512, 512, BF16),
    ("u", 512, 512, BF16), ("v", 512, 512, BF16), ("q", 256, 256, BF16), ("k", 256, 256, BF16),
    ("gv", 512, 512, BF16), ("r", 512, 512, BF16), ("glr", 32, LANE, F32), ("gate", 4096, 4096, BF16),
)
N_IN_PAD = sum(g[2] for g in PROJ_GROUPS)


def _sigmoid(x):
    return 1.0 / (1.0 + jnp.exp(-x))


def _softplus(x):
    return jnp.maximum(x, 0.0) + jnp.log1p(jnp.exp(-jnp.abs(x)))


def _gelu_tanh(x):
    c = 2.0 * np.sqrt(2.0 / np.pi)
    return x * _sigmoid(x * (c + (c * 0.044715) * (x * x)))


def _rms(x, g):
    return x * lax.rsqrt(jnp.mean(x * x, axis=-1, keepdims=True) + RMS_EPS) * g


def _dot(a, b):
    return jnp.dot(a, b, preferred_element_type=F32)


def _dot_tn(a, b):
    return lax.dot_general(a, b, (((0,), (0,)), ((), ())), preferred_element_type=F32)


def _dot_nt(a, b):
    return lax.dot_general(a, b, (((1,), (1,)), ((), ())), preferred_element_type=F32)


def _split_bf16(x):
    hi = x.astype(BF16)
    lo = (x - hi.astype(F32)).astype(BF16)
    return hi, lo


def _dot_split_rhs(a_bf16, x):
    hi, lo = _split_bf16(x)
    return _dot(a_bf16, hi) + _dot(a_bf16, lo)


def _dot_split_lhs(x, b_bf16):
    hi, lo = _split_bf16(x)
    return _dot(hi, b_bf16) + _dot(lo, b_bf16)


def _wspec(arr, layer):
    nd = arr.ndim - 1
    return pl.BlockSpec((None,) + tuple(arr.shape[1:]), lambda *_: (layer,) + (0,) * nd,
                        pipeline_mode=pl.Buffered(1))


def _cspec(arr):
    nd = arr.ndim
    return pl.BlockSpec(tuple(arr.shape), lambda *_: (0,) * nd, pipeline_mode=pl.Buffered(1))


def _params(n_grid):
    return pltpu.CompilerParams(dimension_semantics=("arbitrary",) * n_grid, vmem_limit_bytes=VMEM_LIMIT)


def _in_proj_kernel(x_ref, g_ref, w_ref, *out_refs):
    h = _rms(x_ref[...], g_ref[...]).astype(BF16)
    off = 0
    for o_ref, (name, _, width, dtype) in zip(out_refs, PROJ_GROUPS):
        for c in range(0, width, 512):
            cw = min(512, width - c)
            acc = _dot(h, w_ref[:, off + c:off + c + cw])
            if name == "gate":
                acc = _sigmoid(acc)
            o_ref[:, c:c + cw] = acc.astype(dtype)
        off += width


def _in_proj(x2d, norm_g, w_in_p, layer, tm):
    n = x2d.shape[0]
    out_shape = [jax.ShapeDtypeStruct((n, g[2]), g[3]) for g in PROJ_GROUPS]
    out_specs = [pl.BlockSpec((tm, g[2]), lambda i: (i, 0)) for g in PROJ_GROUPS]
    return pl.pallas_call(
        _in_proj_kernel,
        grid=(n // tm,),
        in_specs=[pl.BlockSpec((tm, D_MODEL), lambda i: (i, 0)), _wspec(norm_g, layer), _wspec(w_in_p, layer)],
        out_specs=out_specs,
        out_shape=out_shape,
        compiler_params=_params(1),
        name="in_proj",
    )(x2d, norm_g, w_in_p)


def _tile_spec(tl, width, n_tiles=None):
    if n_tiles is None:
        return pl.BlockSpec((None, tl, width), lambda b, i: (b, i, 0))
    return pl.BlockSpec((None, tl, width), lambda b, i: (b, n_tiles - 1 - i, 0))


def _chunk_spec(per_tile, rows, width, n_tiles=None):
    if n_tiles is None:
        return pl.BlockSpec((None, per_tile, rows, width), lambda b, i: (b, i, 0, 0))
    return pl.BlockSpec((None, per_tile, rows, width), lambda b, i: (b, n_tiles - 1 - i, 0, 0))


def _halo_specs(tl, width, n_tiles, reverse=False):
    per = tl // HALO
    last = n_tiles * per - 1

    def tile(i):
        return n_tiles - 1 - i if reverse else i

    prev = pl.BlockSpec((None, HALO, width), lambda b, i: (b, jnp.maximum(tile(i) * per - 1, 0), 0))
    nxt = pl.BlockSpec((None, HALO, width), lambda b, i: (b, jnp.minimum((tile(i) + 1) * per, last), 0))
    return prev, nxt


def _fill_halo_buffer(ext_ref, cur_ref, prev_ref, next_ref, is_first, is_last):
    tl = cur_ref.shape[0]
    zeros = jnp.zeros((HALO, ext_ref.shape[1]), ext_ref.dtype)
    ext_ref[0:HALO, :] = jnp.where(is_first, zeros, prev_ref[...])
    ext_ref[HALO:HALO + tl, :] = cur_ref[...]
    ext_ref[HALO + tl:2 * HALO + tl, :] = jnp.where(is_last, zeros, next_ref[...])


HALO_WINDOW = 128 + 2 * HALO


def _band_matrix(offsets):
    r = np.arange(128)[:, None]
    w = np.arange(HALO_WINDOW)[None, :]
    return sum((w == r + HALO + off).astype(np.float32) for off in offsets)


def _ssd_conv(xe_ref, r0, shift_ref, cw_ref, cb_ref):
    t = SSD_CHUNK
    pad = SSD_CONV // 2
    win = xe_ref[pl.ds(r0, HALO_WINDOW), :]
    shifted = _dot(shift_ref[...], win)
    acc = cb_ref[...] + cw_ref[pad:pad + 1, :] * win[HALO:HALO + t, :].astype(F32)
    for n, j in enumerate(j for j in range(SSD_CONV) if j != pad):
        acc = acc + cw_ref[j:j + 1, :] * shifted[n * t:(n + 1) * t, :]
    return acc * _sigmoid(acc)


def _ssd_decay_terms(dt_raw, dtb_ref, alog_ref, tril_ref, triu_ref):
    t = dt_raw.shape[0]
    lane = lax.broadcasted_iota(jnp.int32, (t, LANE), 1)
    used = lane < 2 * SSD_HEADS
    dt = jnp.where(used, _softplus(dt_raw + dtb_ref[...]), 0.0)
    a = jnp.where(used[0:1], -jnp.exp(alog_ref[...]), 0.0)
    dta = dt * a
    cf = _dot_split_rhs(tril_ref[...], dta)
    cr = _dot_split_rhs(triu_ref[...], dta)
    fwd = lane < SSD_HEADS
    acs = jnp.where(fwd, cf, cr)
    tot = jnp.where(fwd[0:1], cf[t - 1:t, :], cr[0:1, :])
    return dt, acs, tot


def _ssd_compact(full):
    half = BRANCH_WIDTH // SSD_GROUPS
    return jnp.concatenate([full[:SSD_STATE, :half], full[SSD_STATE:, half:]], axis=1)


def _ssd_expand(compact):
    half = BRANCH_WIDTH // SSD_GROUPS
    z = jnp.zeros((SSD_STATE, half), compact.dtype)
    return jnp.concatenate([jnp.concatenate([compact[:, :half], z], axis=1),
                            jnp.concatenate([z, compact[:, half:]], axis=1)], axis=0)


def _ssd_local_kernel(xbc_ref, xprev_ref, xnext_ref, dt_ref, cw_ref, cb_ref, alog_ref, dtb_ref, dexp_ref,
                      tril_ref, triu_ref, ef_ref, eb_ref, shift_ref,
                      yp_ref, cs_ref, e_ref, sf_ref, dec_ref, sb_ref, xe_ref, s_ref):
    i = pl.program_id(1)
    n = pl.num_programs(1)
    nci = xbc_ref.shape[0] // SSD_CHUNK
    t = SSD_CHUNK

    @pl.when(i == 0)
    def _():
        s_ref[...] = jnp.zeros_like(s_ref)

    _fill_halo_buffer(xe_ref, xbc_ref, xprev_ref, xnext_ref, i == n - 1, i == 0)

    lane = lax.broadcasted_iota(jnp.int32, (t, LANE), 1)
    low = lane < SSD_STATE
    row = lax.broadcasted_iota(jnp.int32, (t, t), 0)
    col = lax.broadcasted_iota(jnp.int32, (t, t), 1)
    tge = row >= col
    tle = row <= col
    hpg = SSD_HEADS // SSD_GROUPS

    chunks = range(nci)
    xc = [_ssd_conv(xe_ref, c * t, shift_ref, cw_ref, cb_ref) for c in chunks]
    xs = [a[:, :BRANCH_WIDTH] for a in xc]
    bs = [a[:, BRANCH_WIDTH:BRANCH_WIDTH + LANE].astype(BF16) for a in xc]
    cs = [a[:, BRANCH_WIDTH + LANE:] for a in xc]
    terms = [_ssd_decay_terms(dt_ref[pl.ds(c * t, t), :], dtb_ref, alog_ref, tril_ref, triu_ref) for c in chunks]
    dt = [a[0] for a in terms]
    acs = [a[1] for a in terms]
    tot = [a[2] for a in terms]
    cb = [(_dot_nt(jnp.where(low, cs[c], 0.0).astype(BF16), bs[c]),
           _dot_nt(jnp.where(low, 0.0, cs[c]).astype(BF16), bs[c])) for c in chunks]
    acs_t = [a.T for a in acs]
    dt_t = [a.T for a in dt]
    w = [(jnp.exp(tot[c] - acs[c]) * dt[c]).astype(BF16) for c in chunks]
    w_f = [_dot(w[c], ef_ref[...]) for c in chunks]
    w_b = [_dot(w[c], eb_ref[...]) for c in chunks]
    dec8 = [jnp.broadcast_to(jnp.exp(tot[c]), (8, LANE)) for c in chunks]
    dec_f = [_dot_split_lhs(dec8[c], ef_ref[...]) for c in chunks]
    dec_b = [_dot_split_lhs(dec8[c], eb_ref[...]) for c in chunks]

    y_pairs = [[] for _ in chunks]
    for pair in range(SSD_HEADS // 2):
        acc = [None for _ in chunks]
        for sub in range(2):
            h = 2 * pair + sub
            hb = SSD_HEADS + h
            half_mask = low if sub == 0 else jnp.logical_not(low)
            for c in chunks:
                arg = jnp.where(tge, acs[c][:, h:h + 1] - acs_t[c][h:h + 1, :],
                                acs[c][:, hb:hb + 1] - acs_t[c][hb:hb + 1, :])
                wgt = jnp.where(tge, dt_t[c][h:h + 1, :], 0.0) + jnp.where(tle, dt_t[c][hb:hb + 1, :], 0.0)
                m = (cb[c][h // hpg] * jnp.exp(arg) * wgt).astype(BF16)
                x_half = jnp.where(half_mask, xs[c][:, pair * LANE:(pair + 1) * LANE], 0.0).astype(BF16)
                part = _dot(m, x_half)
                acc[c] = part if acc[c] is None else acc[c] + part
        for c in chunks:
            y_pairs[c].append(acc[c])

    st_f = [_ssd_compact(_dot_tn(bs[c], (xs[c] * w_f[c]).astype(BF16))) for c in chunks]
    st_b = [_ssd_compact(_dot_tn(bs[c], (xs[c] * w_b[c]).astype(BF16))) for c in chunks]
    srow = lax.broadcasted_iota(jnp.int32, (8, BRANCH_WIDTH), 0)
    for c in chunks:
        rows = pl.ds(c * t, t)
        y = jnp.concatenate(y_pairs[c], axis=1) + xs[c] * dexp_ref[...]
        yp_ref[rows, :] = y.astype(yp_ref.dtype)
        cs_ref[rows, :] = cs[c].astype(cs_ref.dtype)
        e_ref[rows, :] = jnp.exp(acs[c]).astype(e_ref.dtype)
        dec_ref[c] = jnp.where(srow == 0, dec_f[c], dec_b[c])
        sf_ref[c] = st_f[c]
    for c in reversed(chunks):
        sb_ref[c] = s_ref[...].astype(sb_ref.dtype)
        s_ref[...] = s_ref[...] * dec_b[c][0:1, :] + st_b[c]


def _ssd_final_kernel(yp_ref, cs_ref, e_ref, z_ref, sf_ref, dec_ref, sb_ref, ng_ref, ef_ref, eb_ref, o_ref, s_ref):
    i = pl.program_id(1)
    nci = yp_ref.shape[0] // SSD_CHUNK
    t = SSD_CHUNK

    @pl.when(i == 0)
    def _():
        s_ref[...] = jnp.zeros_like(s_ref)

    half = BRANCH_WIDTH // SSD_GROUPS
    ng = ng_ref[...]
    chunks = range(nci)
    rows = [pl.ds(c * t, t) for c in chunks]
    s_f = []
    for c in chunks:
        s_f.append(s_ref[...].astype(BF16))
        s_ref[...] = s_ref[...] * dec_ref[c][0:1, :] + sf_ref[c]
    ex_f = [_dot(e_ref[rows[c], :], ef_ref[...]) for c in chunks]
    ex_b = [_dot(e_ref[rows[c], :], eb_ref[...]) for c in chunks]
    off_f = [_dot(cs_ref[rows[c], :], _ssd_expand(s_f[c])) for c in chunks]
    off_b = [_dot(cs_ref[rows[c], :], _ssd_expand(sb_ref[c])) for c in chunks]
    for c in chunks:
        y = yp_ref[rows[c], :].astype(F32) + ex_f[c] * off_f[c] + ex_b[c] * off_b[c]
        zf = z_ref[rows[c], :].astype(F32)
        y = y * (zf * _sigmoid(zf))
        y = jnp.concatenate([_rms(y[:, g * half:(g + 1) * half], ng[:, g * half:(g + 1) * half])
                             for g in range(SSD_GROUPS)], axis=1)
        o_ref[rows[c], :] = y.astype(o_ref.dtype)


def _ssd_constants():
    t = SSD_CHUNK
    r = np.arange(t)
    tril = (r[:, None] >= r[None, :]).astype(np.float32)
    triu = (r[:, None] <= r[None, :]).astype(np.float32)
    col = np.arange(BRANCH_WIDTH)
    ef = (r[:, None] == (col[None, :] // SSD_HEAD_DIM)).astype(np.float32)
    eb = (r[:, None] == (col[None, :] // SSD_HEAD_DIM) + SSD_HEADS).astype(np.float32)
    pad = SSD_CONV // 2
    shift = np.concatenate([_band_matrix([j - pad]) for j in range(SSD_CONV) if j != pad], axis=0)
    return [jnp.asarray(a, BF16) for a in (tril, triu, ef, eb, shift)]


def _ssd_mixer(z, xbc, dtp, p, layer, tl):
    b, l, _ = z.shape
    n_tiles = l // tl
    nci = tl // SSD_CHUNK
    nc = l // SSD_CHUNK
    tril, triu, ef, eb, shift = _ssd_constants()
    prev_r, next_r = _halo_specs(tl, SSD_CONV_CH, n_tiles, reverse=True)
    state_shape = jax.ShapeDtypeStruct((b, nc, SSD_STATE, BRANCH_WIDTH), F32)
    yp, cs, e, sf, dec, sb = pl.pallas_call(
        _ssd_local_kernel,
        grid=(b, n_tiles),
        in_specs=[_tile_spec(tl, SSD_CONV_CH, n_tiles), prev_r, next_r, _tile_spec(tl, LANE, n_tiles),
                  _wspec(p["ssd_conv_w"], layer), _wspec(p["ssd_conv_b"], layer), _wspec(p["ssd_a_log"], layer),
                  _wspec(p["ssd_dt_bias"], layer), _wspec(p["ssd_d"], layer),
                  _cspec(tril), _cspec(triu), _cspec(ef), _cspec(eb), _cspec(shift)],
        out_specs=[_tile_spec(tl, BRANCH_WIDTH, n_tiles), _tile_spec(tl, LANE, n_tiles), _tile_spec(tl, LANE, n_tiles),
                   _chunk_spec(nci, SSD_STATE, BRANCH_WIDTH, n_tiles), _chunk_spec(nci, 8, BRANCH_WIDTH, n_tiles),
                   _chunk_spec(nci, SSD_STATE, BRANCH_WIDTH, n_tiles)],
        out_shape=[jax.ShapeDtypeStruct((b, l, BRANCH_WIDTH), BF16), jax.ShapeDtypeStruct((b, l, LANE), BF16),
                   jax.ShapeDtypeStruct((b, l, LANE), BF16), state_shape,
                   jax.ShapeDtypeStruct((b, nc, 8, BRANCH_WIDTH), F32),
                   jax.ShapeDtypeStruct((b, nc, SSD_STATE, BRANCH_WIDTH), BF16)],
        scratch_shapes=[pltpu.VMEM((tl + 2 * HALO, SSD_CONV_CH), BF16), pltpu.VMEM((SSD_STATE, BRANCH_WIDTH), F32)],
        compiler_params=_params(2),
        name="ssd_local",
    )(xbc, xbc, xbc, dtp, p["ssd_conv_w"], p["ssd_conv_b"], p["ssd_a_log"], p["ssd_dt_bias"], p["ssd_d"],
      tril, triu, ef, eb, shift)

    return pl.pallas_call(
        _ssd_final_kernel,
        grid=(b, n_tiles),
        in_specs=[_tile_spec(tl, BRANCH_WIDTH), _tile_spec(tl, LANE), _tile_spec(tl, LANE), _tile_spec(tl, BRANCH_WIDTH),
                  _chunk_spec(nci, SSD_STATE, BRANCH_WIDTH), _chunk_spec(nci, 8, BRANCH_WIDTH),
                  _chunk_spec(nci, SSD_STATE, BRANCH_WIDTH), _wspec(p["ssd_norm"], layer), _cspec(ef), _cspec(eb)],
        out_specs=_tile_spec(tl, BRANCH_WIDTH),
        out_shape=jax.ShapeDtypeStruct((b, l, BRANCH_WIDTH), BF16),
        scratch_shapes=[pltpu.VMEM((SSD_STATE, BRANCH_WIDTH), F32)],
        compiler_params=_params(2),
        name="ssd_final",
    )(yp, cs, e, z, sf, dec, sb, p["ssd_norm"], ef, eb)


def _pool_kernel(p_ref, pprev_ref, pnext_ref, band_ref, w_ref, scale_ref, o_ref, pe_ref, *, seq_len):
    i = pl.program_id(1)
    n = pl.num_programs(1)
    tl = p_ref.shape[0]
    t = POOL_CHUNK
    _fill_halo_buffer(pe_ref, p_ref, pprev_ref, pnext_ref, i == 0, i == n - 1)
    blocks = [(r0, g) for r0 in range(0, tl, t) for g in range(len(POOL_WINDOWS))]
    cols = lambda g: slice(g * POOL_GROUP_DIM, (g + 1) * POOL_GROUP_DIM)
    sums = [_dot(band_ref[g], pe_ref[r0:r0 + HALO_WINDOW, cols(g)]) for r0, g in blocks]
    pooled = []
    for (r0, g), acc in zip(blocks, sums):
        left = POOL_WINDOWS[g] // 2
        right = POOL_WINDOWS[g] - 1 - left
        pos = i * tl + r0 + lax.broadcasted_iota(jnp.int32, (t, 1), 0)
        cnt = (jnp.minimum(pos + right, seq_len - 1) - jnp.maximum(pos - left, 0) + 1).astype(F32)
        pooled.append((acc / cnt - pe_ref[HALO + r0:HALO + r0 + t, cols(g)].astype(F32)).astype(BF16))
    mixed = [_dot(pb, w_ref[g]) for (r0, g), pb in zip(blocks, pooled)]
    for (r0, g), y in zip(blocks, mixed):
        o_ref[r0:r0 + t, cols(g)] = (y * scale_ref[:, cols(g)]).astype(o_ref.dtype)


def _pool_mixer(pin, p, layer, tl):
    b, l, _ = pin.shape
    n_tiles = l // tl
    prev, nxt = _halo_specs(tl, BRANCH_WIDTH, n_tiles)
    band = jnp.asarray(np.stack([_band_matrix(range(-(w // 2), w - w // 2)) for w in POOL_WINDOWS]), BF16)
    return pl.pallas_call(
        functools.partial(_pool_kernel, seq_len=l),
        grid=(b, n_tiles),
        in_specs=[_tile_spec(tl, BRANCH_WIDTH), prev, nxt, _cspec(band), _wspec(p["pool_w"], layer),
                  _wspec(p["pool_scale"], layer)],
        out_specs=_tile_spec(tl, BRANCH_WIDTH),
        out_shape=jax.ShapeDtypeStruct((b, l, BRANCH_WIDTH), BF16),
        scratch_shapes=[pltpu.VMEM((tl + 2 * HALO, BRANCH_WIDTH), BF16)],
        compiler_params=_params(2),
        name="pool",
    )(pin, pin, pin, band, p["pool_w"], p["pool_scale"])


def _sgu_kernel(u_ref, v_ref, ng_ref, w_ref, bias_ref, o_ref):
    tl = u_ref.shape[0]
    gd = BRANCH_WIDTH // SGU_GROUPS
    for r in range(0, tl, SGU_CHUNK):
        rows = slice(r, r + SGU_CHUNK)
        vf = _rms(_gelu_tanh(v_ref[rows, :].astype(F32)), ng_ref[...]).astype(BF16)
        uf = _gelu_tanh(u_ref[rows, :].astype(F32))
        for g in range(SGU_GROUPS):
            cols = slice(g * gd, (g + 1) * gd)
            mixed = _dot(w_ref[g], vf[:, cols]) + bias_ref[:, cols]
            o_ref[rows, cols] = (uf[:, cols] * mixed).astype(o_ref.dtype)


def _sgu_mixer(u, v, p, layer, tl):
    b, l, _ = u.shape
    return pl.pallas_call(
        _sgu_kernel,
        grid=(b, l // tl),
        in_specs=[_tile_spec(tl, BRANCH_WIDTH), _tile_spec(tl, BRANCH_WIDTH), _wspec(p["sgu_norm"], layer),
                  _wspec(p["sgu_w"], layer), _wspec(p["sgu_bias"], layer)],
        out_specs=_tile_spec(tl, BRANCH_WIDTH),
        out_shape=jax.ShapeDtypeStruct((b, l, BRANCH_WIDTH), BF16),
        compiler_params=_params(2),
        name="sgu",
    )(u, v, p["sgu_norm"], p["sgu_w"], p["sgu_bias"])


def _gla_decay_matrix(ecol, col):
    return jnp.concatenate(
        [jnp.broadcast_to(ecol[h * GLA_KEY_DIM:(h + 1) * GLA_KEY_DIM, col:col + 1], (GLA_KEY_DIM, GLA_VAL_DIM))
         for h in range(GLA_HEADS)], axis=1)


def _gla_block_diag(compact):
    z = jnp.zeros((GLA_KEY_DIM, GLA_VAL_DIM), compact.dtype)
    return jnp.concatenate(
        [jnp.concatenate([compact[:, h * GLA_VAL_DIM:(h + 1) * GLA_VAL_DIM] if g == h else z
                          for g in range(GLA_HEADS)], axis=1) for h in range(GLA_HEADS)], axis=0)


def _gla_local_kernel(q_ref, k_ref, v_ref, glr_ref, w2_ref, gb_ref, tril_ref, cmask_ref,
                      oi_ref, qd_ref, sf_ref, ecol_ref, sb_ref, s_ref):
    i = pl.program_id(1)
    t = GLA_CHUNK
    nch = GLA_TILE_CHUNKS
    tile = GLA_TILE

    @pl.when(i == 0)
    def _():
        s_ref[...] = jnp.zeros_like(s_ref)

    pre = _dot(glr_ref[...].astype(BF16), w2_ref[...]) + gb_ref[...]
    gk = -_softplus(-pre) * (1.0 / GLA_GATE_NORMALIZER)
    bc = _dot_split_rhs(tril_ref[...], gk)
    last = [bc[c * t + t - 1:c * t + t, :] for c in range(nch)]
    tot = jnp.concatenate([jnp.broadcast_to(row, (t, 2 * GLA_QK)) for row in last], axis=0)
    bc_f = bc[:, :GLA_QK]
    bc_b = bc[:, GLA_QK:]
    rb = tot[:, GLA_QK:] - bc_b + gk[:, GLA_QK:]
    qf = q_ref[...].astype(F32) * (GLA_KEY_DIM ** -0.5)
    kf = k_ref[...].astype(F32)
    v = v_ref[...]
    qd_f = qf * jnp.exp(bc_f)
    qd_b = qf * jnp.exp(rb)
    ki_f = (kf * jnp.exp(-bc_f)).astype(BF16)
    ki_b = (kf * jnp.exp(-rb)).astype(BF16)
    ke_f = kf * jnp.exp(tot[:, :GLA_QK] - bc_f)
    ke_b = kf * jnp.exp(bc_b - gk[:, GLA_QK:])
    qd_ref[:, :GLA_QK] = qd_f.astype(qd_ref.dtype)
    qd_ref[:, GLA_QK:] = qd_b.astype(qd_ref.dtype)

    lane = lax.broadcasted_iota(jnp.int32, (tile, GLA_QK), 1)

    def stack_heads(qd):
        return jnp.concatenate(
            [jnp.where((lane >= h * GLA_KEY_DIM) & (lane < (h + 1) * GLA_KEY_DIM), qd, 0.0)
             for h in range(GLA_HEADS)], axis=0).astype(BF16)

    att_f = _dot_nt(stack_heads(qd_f), ki_f)
    att_b = _dot_nt(stack_heads(qd_b), ki_b)
    row = lax.broadcasted_iota(jnp.int32, (tile, tile), 0)
    col = lax.broadcasted_iota(jnp.int32, (tile, tile), 1)
    same = (row // t) == (col // t)
    m_f = same & (row >= col)
    m_b = same & (row <= col)
    for h in range(GLA_HEADS):
        rows = slice(h * tile, (h + 1) * tile)
        att = (jnp.where(m_f, att_f[rows, :], 0.0) + jnp.where(m_b, att_b[rows, :], 0.0)).astype(BF16)
        oi_ref[:, h * GLA_VAL_DIM:(h + 1) * GLA_VAL_DIM] = _dot(
            att, v[:, h * GLA_VAL_DIM:(h + 1) * GLA_VAL_DIM]).astype(oi_ref.dtype)

    ket_f = ke_f.T.astype(BF16)
    ket_b = ke_b.T.astype(BF16)
    cmask = cmask_ref[...]
    st = []
    for h in range(GLA_HEADS):
        vh = v[:, h * GLA_VAL_DIM:(h + 1) * GLA_VAL_DIM]
        v_exp = jnp.concatenate([vh] * nch, axis=1) * cmask
        lhs = jnp.concatenate([ket_f[h * GLA_KEY_DIM:(h + 1) * GLA_KEY_DIM, :],
                               ket_b[h * GLA_KEY_DIM:(h + 1) * GLA_KEY_DIM, :]], axis=0)
        st.append(_dot(lhs, v_exp))

    def local_state(c, lo):
        return jnp.concatenate([st[h][lo:lo + GLA_KEY_DIM, c * GLA_VAL_DIM:(c + 1) * GLA_VAL_DIM]
                                for h in range(GLA_HEADS)], axis=1)

    totm = jnp.concatenate([row[:, :GLA_QK] for row in last] + [row[:, GLA_QK:] for row in last]
                           + [jnp.zeros((LANE - 2 * nch, GLA_QK), F32)], axis=0)
    ecol = jnp.exp(totm.T)
    ecol_ref[...] = ecol
    for c in reversed(range(nch)):
        sf_ref[c] = local_state(c, 0)
        sb_ref[c] = s_ref[...].astype(sb_ref.dtype)
        s_ref[...] = s_ref[...] * _gla_decay_matrix(ecol, nch + c) + local_state(c, GLA_KEY_DIM)


def _gla_final_kernel(oi_ref, qd_ref, sf_ref, ecol_ref, sb_ref, r_ref, ng_ref, o_ref, s_ref):
    i = pl.program_id(1)
    t = GLA_CHUNK

    @pl.when(i == 0)
    def _():
        s_ref[...] = jnp.zeros_like(s_ref)

    ecol = ecol_ref[...]
    ng = ng_ref[...]
    chunks = range(GLA_TILE_CHUNKS)
    rows = [pl.ds(c * t, t) for c in chunks]
    s_f = []
    for c in chunks:
        s_f.append(s_ref[...].astype(BF16))
        s_ref[...] = s_ref[...] * _gla_decay_matrix(ecol, c) + sf_ref[c]
    inter_f = [_dot(qd_ref[rows[c], :GLA_QK], _gla_block_diag(s_f[c])) for c in chunks]
    inter_b = [_dot(qd_ref[rows[c], GLA_QK:], _gla_block_diag(sb_ref[c])) for c in chunks]
    o = jnp.concatenate([oi_ref[rows[c], :].astype(F32) + inter_f[c] + inter_b[c] for c in chunks], axis=0)
    rf = r_ref[...].astype(F32)
    o = jnp.concatenate([_rms(o[:, h * GLA_VAL_DIM:(h + 1) * GLA_VAL_DIM], ng[:, h * GLA_VAL_DIM:(h + 1) * GLA_VAL_DIM])
                         for h in range(GLA_HEADS)], axis=1)
    o_ref[...] = (o * (rf * _sigmoid(rf))).astype(o_ref.dtype)


def _gla_constants():
    r = np.arange(GLA_TILE)
    tril = ((r[:, None] >= r[None, :]) & (r[:, None] // GLA_CHUNK == r[None, :] // GLA_CHUNK)).astype(np.float32)
    ccol = np.arange(GLA_TILE_CHUNKS * GLA_VAL_DIM) // GLA_VAL_DIM
    cmask = (r[:, None] // GLA_CHUNK == ccol[None, :]).astype(np.float32)
    return jnp.asarray(tril, BF16), jnp.asarray(cmask, BF16)


def _gla_mixer(q, k, v, r, glr, p, layer):
    b, l, _ = q.shape
    tl = GLA_TILE
    n_tiles = l // tl
    nch = GLA_TILE_CHUNKS
    nc = l // GLA_CHUNK
    tril, cmask = _gla_constants()
    oi, qd, sf, ecol, sb = pl.pallas_call(
        _gla_local_kernel,
        grid=(b, n_tiles),
        in_specs=[_tile_spec(tl, GLA_QK, n_tiles), _tile_spec(tl, GLA_QK, n_tiles), _tile_spec(tl, BRANCH_WIDTH, n_tiles),
                  _tile_spec(tl, LANE, n_tiles), _wspec(p["gla_w2"], layer), _wspec(p["gla_gate_b"], layer),
                  _cspec(tril), _cspec(cmask)],
        out_specs=[_tile_spec(tl, BRANCH_WIDTH, n_tiles), _tile_spec(tl, 2 * GLA_QK, n_tiles),
                   _chunk_spec(nch, GLA_KEY_DIM, BRANCH_WIDTH, n_tiles), _tile_spec(GLA_QK, LANE, n_tiles),
                   _chunk_spec(nch, GLA_KEY_DIM, BRANCH_WIDTH, n_tiles)],
        out_shape=[jax.ShapeDtypeStruct((b, l, BRANCH_WIDTH), BF16), jax.ShapeDtypeStruct((b, l, 2 * GLA_QK), BF16),
                   jax.ShapeDtypeStruct((b, nc, GLA_KEY_DIM, BRANCH_WIDTH), F32),
                   jax.ShapeDtypeStruct((b, n_tiles * GLA_QK, LANE), F32),
                   jax.ShapeDtypeStruct((b, nc, GLA_KEY_DIM, BRANCH_WIDTH), BF16)],
        scratch_shapes=[pltpu.VMEM((GLA_KEY_DIM, BRANCH_WIDTH), F32)],
        compiler_params=_params(2),
        name="gla_local",
    )(q, k, v, glr, p["gla_w2"], p["gla_gate_b"], tril, cmask)

    return pl.pallas_call(
        _gla_final_kernel,
        grid=(b, n_tiles),
        in_specs=[_tile_spec(tl, BRANCH_WIDTH), _tile_spec(tl, 2 * GLA_QK), _chunk_spec(nch, GLA_KEY_DIM, BRANCH_WIDTH),
                  _tile_spec(GLA_QK, LANE), _chunk_spec(nch, GLA_KEY_DIM, BRANCH_WIDTH), _tile_spec(tl, BRANCH_WIDTH),
                  _wspec(p["gla_norm"], layer)],
        out_specs=_tile_spec(tl, BRANCH_WIDTH),
        out_shape=jax.ShapeDtypeStruct((b, l, BRANCH_WIDTH), BF16),
        scratch_shapes=[pltpu.VMEM((GLA_KEY_DIM, BRANCH_WIDTH), F32)],
        compiler_params=_params(2),
        name="gla_final",
    )(oi, qd, sf, ecol, sb, r, p["gla_norm"])


def _merge_ffn_kernel(x_ref, b0_ref, b1_ref, b2_ref, b3_ref, gate_ref, wb_ref, wo_ref, gpost_ref, gpre_ref,
                      w1_ref, w2_ref, gffn_ref, o_ref):
    merged = None
    for n, b_ref in enumerate((b0_ref, b1_ref, b2_ref, b3_ref)):
        term = gate_ref[:, n * D_MODEL:(n + 1) * D_MODEL].astype(F32) * _dot(b_ref[...], wb_ref[n])
        merged = term if merged is None else merged + term
    x1 = x_ref[...] + _rms(_dot(merged.astype(BF16), wo_ref[...]), gpost_ref[...])
    h2 = _rms(x1, gpre_ref[...]).astype(BF16)
    ff = None
    for c in range(0, D_FF, D_MODEL):
        hid = jnp.square(jnp.maximum(_dot(h2, w1_ref[:, c:c + D_MODEL]), 0.0)).astype(BF16)
        part = _dot(hid, w2_ref[c:c + D_MODEL, :])
        ff = part if ff is None else ff + part
    o_ref[...] = x1 + _rms(ff, gffn_ref[...])


def _merge_ffn(x2d, branches, gate, p, layer, tm):
    n = x2d.shape[0]
    row = lambda w: pl.BlockSpec((tm, w), lambda i: (i, 0))
    return pl.pallas_call(
        _merge_ffn_kernel,
        grid=(n // tm,),
        in_specs=[row(D_MODEL)] + [row(BRANCH_WIDTH)] * N_BRANCH + [row(N_BRANCH * D_MODEL)] + [
            _wspec(p["w_branch"], layer), _wspec(p["w_out"], layer), _wspec(p["norm_mix_post"], layer),
            _wspec(p["norm_ffn_pre"], layer), _wspec(p["w_ff1"], layer), _wspec(p["w_ff2"], layer),
            _wspec(p["norm_ffn_post"], layer)],
        out_specs=row(D_MODEL),
        out_shape=jax.ShapeDtypeStruct((n, D_MODEL), F32),
        compiler_params=_params(1),
        name="merge_ffn",
    )(x2d, *branches, gate, p["w_branch"], p["w_out"], p["norm_mix_post"], p["norm_ffn_pre"], p["w_ff1"],
      p["w_ff2"], p["norm_ffn_post"])


def _pad_lanes(a, width):
    return jnp.pad(a, [(0, 0)] * (a.ndim - 1) + [(0, width - a.shape[-1])])


def _prepare_params(norm_mix_pre, w_in, ssd_conv_w, ssd_conv_b, ssd_a_log, ssd_dt_bias, ssd_d, ssd_norm, pool_w,
                    pool_scale, sgu_norm, sgu_w, sgu_b, gla_gate_w2, gla_gate_b, gla_norm, w_branch, w_out,
                    norm_mix_post, norm_ffn_pre, w_ff1, w_ff2, norm_ffn_post):
    depth = w_in.shape[0]
    pieces, off = [], 0
    for _, width, padded, _ in PROJ_GROUPS:
        pieces.append(_pad_lanes(w_in[:, :, off:off + width], padded))
        off += width
    row = lambda a: a.reshape(depth, 1, -1).astype(F32)
    w2 = jnp.zeros((depth, LANE, 2 * GLA_QK), F32)
    w2 = w2.at[:, :GLA_GATE_RANK, :GLA_QK].set(gla_gate_w2[:, 0])
    w2 = w2.at[:, GLA_GATE_RANK:2 * GLA_GATE_RANK, GLA_QK:].set(gla_gate_w2[:, 1])
    return {
        "norm_mix_pre": row(norm_mix_pre),
        "w_in": jnp.concatenate(pieces, axis=-1).astype(BF16),
        "ssd_conv_w": jnp.pad(ssd_conv_w.astype(F32), [(0, 0), (0, 8 - SSD_CONV), (0, 0)]),
        "ssd_conv_b": row(ssd_conv_b),
        "ssd_a_log": _pad_lanes(row(ssd_a_log), LANE),
        "ssd_dt_bias": _pad_lanes(row(ssd_dt_bias), LANE),
        "ssd_d": row(jnp.repeat(ssd_d, SSD_HEAD_DIM, axis=-1)),
        "ssd_norm": row(ssd_norm),
        "pool_w": pool_w.astype(BF16),
        "pool_scale": row(pool_scale),
        "sgu_norm": row(sgu_norm),
        "sgu_w": sgu_w.astype(BF16),
        "sgu_bias": jnp.repeat(jnp.swapaxes(sgu_b, 1, 2), BRANCH_WIDTH // SGU_GROUPS, axis=-1).astype(F32),
        "gla_w2": w2.astype(BF16),
        "gla_gate_b": row(gla_gate_b),
        "gla_norm": row(gla_norm),
        "w_branch": w_branch.astype(BF16),
        "w_out": w_out.astype(BF16),
        "norm_mix_post": row(norm_mix_post),
        "norm_ffn_pre": row(norm_ffn_pre),
        "w_ff1": w_ff1.astype(BF16),
        "w_ff2": w_ff2.astype(BF16),
        "norm_ffn_post": row(norm_ffn_post),
    }


def _tiles(seq_len):
    return min(512, seq_len), min(256, seq_len), min(256, seq_len), min(512, seq_len)


def _trunk(x, p):
    b, l, d = x.shape
    tm_proj, tm_ffn, tl_ssd, tl_local = _tiles(l)
    x2d = x.reshape(b * l, d)
    for layer in range(DEPTH):
        proj = dict(zip([g[0] for g in PROJ_GROUPS], _in_proj(x2d, p["norm_mix_pre"], p["w_in"], layer, tm_proj)))
        seq = lambda name: proj[name].reshape(b, l, -1)
        branches = (
            _ssd_mixer(seq("z"), seq("xbc"), seq("dt"), p, layer, tl_ssd),
            _pool_mixer(seq("pool"), p, layer, tl_local),
            _sgu_mixer(seq("u"), seq("v"), p, layer, tl_local),
            _gla_mixer(seq("q"), seq("k"), seq("gv"), seq("r"), seq("glr"), p, layer),
        )
        branches = [br.reshape(b * l, BRANCH_WIDTH) for br in branches]
        x2d = _merge_ffn(x2d, branches, proj["gate"], p, layer, tm_ffn)
    return x2d.reshape(b, l, d)


def kernel(x_prompt, x_sample, norm_mix_pre, w_in, ssd_conv_w, ssd_conv_b, ssd_a_log, ssd_dt_bias, ssd_d, ssd_norm, pool_w, pool_scale, sgu_norm, sgu_w, sgu_b, gla_gate_w2, gla_gate_b, gla_norm, w_branch, w_out, norm_mix_post, norm_ffn_pre, w_ff1, w_ff2, norm_ffn_post):
    p = _prepare_params(norm_mix_pre, w_in, ssd_conv_w, ssd_conv_b, ssd_a_log, ssd_dt_bias, ssd_d, ssd_norm, pool_w,
                        pool_scale, sgu_norm, sgu_w, sgu_b, gla_gate_w2, gla_gate_b, gla_norm, w_branch, w_out,
                        norm_mix_post, norm_ffn_pre, w_ff1, w_ff2, norm_ffn_post)
    return (_trunk(x_prompt, p), _trunk(x_sample, p))
```

```python
import functools

import jax
import jax.numpy as jnp
import numpy as np
from jax import lax
from jax.experimental import pallas as pl
from jax.experimental.pallas import tpu as pltpu

F32 = jnp.float32
BF16 = jnp.bfloat16

D_MODEL = 1024
DEPTH = 4
BRANCH_WIDTH = 512
N_BRANCH = 4
SSD_HEADS = 8
SSD_HEAD_DIM = 64
SSD_GROUPS = 2
SSD_STATE = 64
SSD_CONV = 5
SSD_CHUNK = 128
SSD_CONV_CH = 768
POOL_WINDOWS = (2, 4, 8, 16)
POOL_GROUP_DIM = 128
POOL_CHUNK = 128
SGU_CHUNK = 128
SGU_GROUPS = 4
GLA_HEADS = 4
GLA_KEY_DIM = 64
GLA_VAL_DIM = 128
GLA_GATE_RANK = 16
GLA_GATE_NORMALIZER = 16.0
GLA_CHUNK = 64
GLA_QK = GLA_HEADS * GLA_KEY_DIM
GLA_TILE = 256
GLA_TILE_CHUNKS = GLA_TILE // GLA_CHUNK
D_FF = 4096
RMS_EPS = 1e-6

LANE = 128
HALO = 16
VMEM_LIMIT = 56 * 1024 * 1024

PROJ_GROUPS = (
    ("z", 512, 512, BF16), ("xbc", 768, 768, BF16), ("dt", 16, LANE, F32), ("pool", 512, 512, BF16),
    ("u", 512, 512, BF16), ("v", 512, 512, BF16), ("q", 256, 256, BF16), ("k", 256, 256, BF16),
    ("gv", 512, 512, BF16), ("r", 512, 512, BF16), ("glr", 32, LANE, F32), ("gate", 4096, 4096, BF16),
)
N_IN_PAD = sum(g[2] for g in PROJ_GROUPS)


LOG2_E = float(np.log2(np.e))


def _sigmoid(x):
    return 1.0 / (1.0 + jnp.exp2(x * (-LOG2_E)))


def _softplus(x):
    return jnp.maximum(x, 0.0) + jnp.log(1.0 + jnp.exp(-jnp.abs(x)))


def _gelu_tanh(x):
    c = -2.0 * np.sqrt(2.0 / np.pi) * LOG2_E
    return x / (1.0 + jnp.exp2(x * (c + (c * 0.044715) * (x * x))))


def _rms(x, g):
    return x * lax.rsqrt(jnp.mean(x * x, axis=-1, keepdims=True) + RMS_EPS) * g


def _dot(a, b):
    return jnp.dot(a, b, preferred_element_type=F32)


def _dot_tn(a, b):
    return lax.dot_general(a, b, (((0,), (0,)), ((), ())), preferred_element_type=F32)


def _dot_nt(a, b):
    return lax.dot_general(a, b, (((1,), (1,)), ((), ())), preferred_element_type=F32)


def _split_bf16(x):
    hi = x.astype(BF16)
    lo = (x - hi.astype(F32)).astype(BF16)
    return hi, lo


def _dot_split_rhs(a_bf16, x):
    hi, lo = _split_bf16(x)
    return _dot(a_bf16, hi) + _dot(a_bf16, lo)


def _dot_split_lhs(x, b_bf16):
    hi, lo = _split_bf16(x)
    return _dot(hi, b_bf16) + _dot(lo, b_bf16)


def _wspec(arr, layer):
    nd = arr.ndim - 1
    return pl.BlockSpec((None,) + tuple(arr.shape[1:]), lambda *_: (layer,) + (0,) * nd,
                        pipeline_mode=pl.Buffered(1))


def _cspec(arr):
    nd = arr.ndim
    return pl.BlockSpec(tuple(arr.shape), lambda *_: (0,) * nd, pipeline_mode=pl.Buffered(1))


def _params(n_grid):
    return pltpu.CompilerParams(dimension_semantics=("arbitrary",) * n_grid, vmem_limit_bytes=VMEM_LIMIT)


def _in_proj_kernel(x_ref, g_ref, w_ref, *out_refs):
    h = _rms(x_ref[...], g_ref[...]).astype(BF16)
    off = 0
    for o_ref, (name, _, width, dtype) in zip(out_refs, PROJ_GROUPS):
        for c in range(0, width, 512):
            cw = min(512, width - c)
            acc = _dot(h, w_ref[:, off + c:off + c + cw])
            if name == "gate":
                acc = _sigmoid(acc)
            o_ref[:, c:c + cw] = acc.astype(dtype)
        off += width


def _in_proj(x2d, norm_g, w_in_p, layer, tm):
    n = x2d.shape[0]
    out_shape = [jax.ShapeDtypeStruct((n, g[2]), g[3]) for g in PROJ_GROUPS]
    out_specs = [pl.BlockSpec((tm, g[2]), lambda i: (i, 0)) for g in PROJ_GROUPS]
    return pl.pallas_call(
        _in_proj_kernel,
        grid=(n // tm,),
        in_specs=[pl.BlockSpec((tm, D_MODEL), lambda i: (i, 0)), _wspec(norm_g, layer), _wspec(w_in_p, layer)],
        out_specs=out_specs,
        out_shape=out_shape,
        compiler_params=_params(1),
        name="in_proj",
    )(x2d, norm_g, w_in_p)


def _tile_spec(tl, width, n_tiles=None):
    if n_tiles is None:
        return pl.BlockSpec((None, tl, width), lambda b, i: (b, i, 0))
    return pl.BlockSpec((None, tl, width), lambda b, i: (b, n_tiles - 1 - i, 0))


def _chunk_spec(per_tile, rows, width, n_tiles=None):
    if n_tiles is None:
        return pl.BlockSpec((None, per_tile, rows, width), lambda b, i: (b, i, 0, 0))
    return pl.BlockSpec((None, per_tile, rows, width), lambda b, i: (b, n_tiles - 1 - i, 0, 0))


def _halo_specs(tl, width, n_tiles, reverse=False):
    per = tl // HALO
    last = n_tiles * per - 1

    def tile(i):
        return n_tiles - 1 - i if reverse else i

    prev = pl.BlockSpec((None, HALO, width), lambda b, i: (b, jnp.maximum(tile(i) * per - 1, 0), 0))
    nxt = pl.BlockSpec((None, HALO, width), lambda b, i: (b, jnp.minimum((tile(i) + 1) * per, last), 0))
    return prev, nxt


def _fill_halo_buffer(ext_ref, cur_ref, prev_ref, next_ref, is_first, is_last):
    tl = cur_ref.shape[0]
    zeros = jnp.zeros((HALO, ext_ref.shape[1]), ext_ref.dtype)
    ext_ref[0:HALO, :] = jnp.where(is_first, zeros, prev_ref[...])
    ext_ref[HALO:HALO + tl, :] = cur_ref[...]
    ext_ref[HALO + tl:2 * HALO + tl, :] = jnp.where(is_last, zeros, next_ref[...])


HALO_WINDOW = 128 + 2 * HALO


def _band_matrix(offsets):
    r = np.arange(128)[:, None]
    w = np.arange(HALO_WINDOW)[None, :]
    return sum((w == r + HALO + off).astype(np.float32) for off in offsets)


def _ssd_conv(xe_ref, r0, shift_ref, cw_ref, cb_ref):
    t = SSD_CHUNK
    pad = SSD_CONV // 2
    win = xe_ref[pl.ds(r0, HALO_WINDOW), :]
    shifted = _dot(shift_ref[...], win)
    acc = cb_ref[...] + cw_ref[pad:pad + 1, :] * win[HALO:HALO + t, :].astype(F32)
    for n, j in enumerate(j for j in range(SSD_CONV) if j != pad):
        acc = acc + cw_ref[j:j + 1, :] * shifted[n * t:(n + 1) * t, :]
    return acc * _sigmoid(acc)


def _ssd_decay_terms(dt_raw, dtb_ref, alog_ref, tril_ref, triu_ref):
    t = dt_raw.shape[0]
    lane = lax.broadcasted_iota(jnp.int32, (t, LANE), 1)
    used = lane < 2 * SSD_HEADS
    dt = jnp.where(used, _softplus(dt_raw + dtb_ref[...]), 0.0)
    a = jnp.where(used[0:1], -jnp.exp(alog_ref[...]), 0.0)
    dta = dt * a
    cf = _dot_split_rhs(tril_ref[...], dta)
    cr = _dot_split_rhs(triu_ref[...], dta)
    fwd = lane < SSD_HEADS
    acs = jnp.where(fwd, cf, cr)
    tot = jnp.where(fwd[0:1], cf[t - 1:t, :], cr[0:1, :])
    return dt, acs, tot


def _ssd_compact(full):
    half = BRANCH_WIDTH // SSD_GROUPS
    return jnp.concatenate([full[:SSD_STATE, :half], full[SSD_STATE:, half:]], axis=1)


def _ssd_expand(compact):
    half = BRANCH_WIDTH // SSD_GROUPS
    z = jnp.zeros((SSD_STATE, half), compact.dtype)
    return jnp.concatenate([jnp.concatenate([compact[:, :half], z], axis=1),
                            jnp.concatenate([z, compact[:, half:]], axis=1)], axis=0)


def _ssd_local_kernel(xbc_ref, xprev_ref, xnext_ref, dt_ref, cw_ref, cb_ref, alog_ref, dtb_ref, dexp_ref,
                      tril_ref, triu_ref, ef_ref, eb_ref, shift_ref,
                      yp_ref, cs_ref, e_ref, sf_ref, dec_ref, sb_ref, xe_ref, s_ref):
    i = pl.program_id(1)
    n = pl.num_programs(1)
    nci = xbc_ref.shape[0] // SSD_CHUNK
    t = SSD_CHUNK

    @pl.when(i == 0)
    def _():
        s_ref[...] = jnp.zeros_like(s_ref)

    _fill_halo_buffer(xe_ref, xbc_ref, xprev_ref, xnext_ref, i == n - 1, i == 0)

    lane = lax.broadcasted_iota(jnp.int32, (t, LANE), 1)
    low = lane < SSD_STATE
    row = lax.broadcasted_iota(jnp.int32, (t, t), 0)
    col = lax.broadcasted_iota(jnp.int32, (t, t), 1)
    tge = row >= col
    tle = row <= col
    hpg = SSD_HEADS // SSD_GROUPS

    chunks = range(nci)
    xc = [_ssd_conv(xe_ref, c * t, shift_ref, cw_ref, cb_ref) for c in chunks]
    xs = [a[:, :BRANCH_WIDTH] for a in xc]
    bs = [a[:, BRANCH_WIDTH:BRANCH_WIDTH + LANE].astype(BF16) for a in xc]
    cs = [a[:, BRANCH_WIDTH + LANE:] for a in xc]
    terms = [_ssd_decay_terms(dt_ref[pl.ds(c * t, t), :], dtb_ref, alog_ref, tril_ref, triu_ref) for c in chunks]
    dt = [a[0] for a in terms]
    acs = [a[1] for a in terms]
    tot = [a[2] for a in terms]
    cb = [(_dot_nt(jnp.where(low, cs[c], 0.0).astype(BF16), bs[c]),
           _dot_nt(jnp.where(low, 0.0, cs[c]).astype(BF16), bs[c])) for c in chunks]
    acs_t = [a.T for a in acs]
    dt_t = [a.T for a in dt]
    w = [(jnp.exp(tot[c] - acs[c]) * dt[c]).astype(BF16) for c in chunks]
    w_f = [_dot(w[c], ef_ref[...]) for c in chunks]
    w_b = [_dot(w[c], eb_ref[...]) for c in chunks]
    dec8 = [jnp.broadcast_to(jnp.exp(tot[c]), (8, LANE)) for c in chunks]
    dec_f = [_dot_split_lhs(dec8[c], ef_ref[...]) for c in chunks]
    dec_b = [_dot_split_lhs(dec8[c], eb_ref[...]) for c in chunks]

    y_pairs = [[] for _ in chunks]
    for pair in range(SSD_HEADS // 2):
        acc = [None for _ in chunks]
        for sub in range(2):
            h = 2 * pair + sub
            hb = SSD_HEADS + h
            half_mask = low if sub == 0 else jnp.logical_not(low)
            for c in chunks:
                arg = jnp.where(tge, acs[c][:, h:h + 1] - acs_t[c][h:h + 1, :],
                                acs[c][:, hb:hb + 1] - acs_t[c][hb:hb + 1, :])
                wgt = jnp.where(tge, dt_t[c][h:h + 1, :], 0.0) + jnp.where(tle, dt_t[c][hb:hb + 1, :], 0.0)
                m = (cb[c][h // hpg] * jnp.exp(arg) * wgt).astype(BF16)
                x_half = jnp.where(half_mask, xs[c][:, pair * LANE:(pair + 1) * LANE], 0.0).astype(BF16)
                part = _dot(m, x_half)
                acc[c] = part if acc[c] is None else acc[c] + part
        for c in chunks:
            y_pairs[c].append(acc[c])

    st_f = [_ssd_compact(_dot_tn(bs[c], (xs[c] * w_f[c]).astype(BF16))) for c in chunks]
    st_b = [_ssd_compact(_dot_tn(bs[c], (xs[c] * w_b[c]).astype(BF16))) for c in chunks]
    srow = lax.broadcasted_iota(jnp.int32, (8, BRANCH_WIDTH), 0)
    for c in chunks:
        rows = pl.ds(c * t, t)
        y = jnp.concatenate(y_pairs[c], axis=1) + xs[c] * dexp_ref[...]
        yp_ref[rows, :] = y.astype(yp_ref.dtype)
        cs_ref[rows, :] = cs[c].astype(cs_ref.dtype)
        e_ref[rows, :] = jnp.exp(acs[c]).astype(e_ref.dtype)
        dec_ref[c] = jnp.where(srow == 0, dec_f[c], dec_b[c])
        sf_ref[c] = st_f[c].astype(sf_ref.dtype)
    for c in reversed(chunks):
        sb_ref[c] = s_ref[...].astype(sb_ref.dtype)
        s_ref[...] = s_ref[...] * dec_b[c][0:1, :] + st_b[c]


def _ssd_final_kernel(yp_ref, cs_ref, e_ref, z_ref, sf_ref, dec_ref, sb_ref, ng_ref, ef_ref, eb_ref, o_ref, s_ref):
    i = pl.program_id(1)
    nci = yp_ref.shape[0] // SSD_CHUNK
    t = SSD_CHUNK

    @pl.when(i == 0)
    def _():
        s_ref[...] = jnp.zeros_like(s_ref)

    half = BRANCH_WIDTH // SSD_GROUPS
    ng = ng_ref[...]
    chunks = range(nci)
    rows = [pl.ds(c * t, t) for c in chunks]
    s_f = []
    for c in chunks:
        s_f.append(s_ref[...].astype(BF16))
        s_ref[...] = s_ref[...] * dec_ref[c][0:1, :] + sf_ref[c].astype(F32)
    ex_f = [_dot(e_ref[rows[c], :], ef_ref[...]) for c in chunks]
    ex_b = [_dot(e_ref[rows[c], :], eb_ref[...]) for c in chunks]
    off_f = [_dot(cs_ref[rows[c], :], _ssd_expand(s_f[c])) for c in chunks]
    off_b = [_dot(cs_ref[rows[c], :], _ssd_expand(sb_ref[c])) for c in chunks]
    for c in chunks:
        y = yp_ref[rows[c], :].astype(F32) + ex_f[c] * off_f[c] + ex_b[c] * off_b[c]
        zf = z_ref[rows[c], :].astype(F32)
        y = y * (zf * _sigmoid(zf))
        y = jnp.concatenate([_rms(y[:, g * half:(g + 1) * half], ng[:, g * half:(g + 1) * half])
                             for g in range(SSD_GROUPS)], axis=1)
        o_ref[rows[c], :] = y.astype(o_ref.dtype)


def _ssd_constants():
    t = SSD_CHUNK
    r = np.arange(t)
    tril = (r[:, None] >= r[None, :]).astype(np.float32)
    triu = (r[:, None] <= r[None, :]).astype(np.float32)
    col = np.arange(BRANCH_WIDTH)
    ef = (r[:, None] == (col[None, :] // SSD_HEAD_DIM)).astype(np.float32)
    eb = (r[:, None] == (col[None, :] // SSD_HEAD_DIM) + SSD_HEADS).astype(np.float32)
    pad = SSD_CONV // 2
    shift = np.concatenate([_band_matrix([j - pad]) for j in range(SSD_CONV) if j != pad], axis=0)
    return [jnp.asarray(a, BF16) for a in (tril, triu, ef, eb, shift)]


def _ssd_mixer(z, xbc, dtp, p, layer, tl):
    b, l, _ = z.shape
    n_tiles = l // tl
    nci = tl // SSD_CHUNK
    nc = l // SSD_CHUNK
    tril, triu, ef, eb, shift = _ssd_constants()
    prev_r, next_r = _halo_specs(tl, SSD_CONV_CH, n_tiles, reverse=True)
    state_shape = jax.ShapeDtypeStruct((b, nc, SSD_STATE, BRANCH_WIDTH), BF16)
    yp, cs, e, sf, dec, sb = pl.pallas_call(
        _ssd_local_kernel,
        grid=(b, n_tiles),
        in_specs=[_tile_spec(tl, SSD_CONV_CH, n_tiles), prev_r, next_r, _tile_spec(tl, LANE, n_tiles),
                  _wspec(p["ssd_conv_w"], layer), _wspec(p["ssd_conv_b"], layer), _wspec(p["ssd_a_log"], layer),
                  _wspec(p["ssd_dt_bias"], layer), _wspec(p["ssd_d"], layer),
                  _cspec(tril), _cspec(triu), _cspec(ef), _cspec(eb), _cspec(shift)],
        out_specs=[_tile_spec(tl, BRANCH_WIDTH, n_tiles), _tile_spec(tl, LANE, n_tiles), _tile_spec(tl, LANE, n_tiles),
                   _chunk_spec(nci, SSD_STATE, BRANCH_WIDTH, n_tiles), _chunk_spec(nci, 8, BRANCH_WIDTH, n_tiles),
                   _chunk_spec(nci, SSD_STATE, BRANCH_WIDTH, n_tiles)],
        out_shape=[jax.ShapeDtypeStruct((b, l, BRANCH_WIDTH), BF16), jax.ShapeDtypeStruct((b, l, LANE), BF16),
                   jax.ShapeDtypeStruct((b, l, LANE), BF16), state_shape,
                   jax.ShapeDtypeStruct((b, nc, 8, BRANCH_WIDTH), F32),
                   jax.ShapeDtypeStruct((b, nc, SSD_STATE, BRANCH_WIDTH), BF16)],
        scratch_shapes=[pltpu.VMEM((tl + 2 * HALO, SSD_CONV_CH), BF16), pltpu.VMEM((SSD_STATE, BRANCH_WIDTH), F32)],
        compiler_params=_params(2),
        name="ssd_local",
    )(xbc, xbc, xbc, dtp, p["ssd_conv_w"], p["ssd_conv_b"], p["ssd_a_log"], p["ssd_dt_bias"], p["ssd_d"],
      tril, triu, ef, eb, shift)

    return pl.pallas_call(
        _ssd_final_kernel,
        grid=(b, n_tiles),
        in_specs=[_tile_spec(tl, BRANCH_WIDTH), _tile_spec(tl, LANE), _tile_spec(tl, LANE), _tile_spec(tl, BRANCH_WIDTH),
                  _chunk_spec(nci, SSD_STATE, BRANCH_WIDTH), _chunk_spec(nci, 8, BRANCH_WIDTH),
                  _chunk_spec(nci, SSD_STATE, BRANCH_WIDTH), _wspec(p["ssd_norm"], layer), _cspec(ef), _cspec(eb)],
        out_specs=_tile_spec(tl, BRANCH_WIDTH),
        out_shape=jax.ShapeDtypeStruct((b, l, BRANCH_WIDTH), BF16),
        scratch_shapes=[pltpu.VMEM((SSD_STATE, BRANCH_WIDTH), F32)],
        compiler_params=_params(2),
        name="ssd_final",
    )(yp, cs, e, z, sf, dec, sb, p["ssd_norm"], ef, eb)


def _pool_kernel(p_ref, pprev_ref, pnext_ref, band_ref, w_ref, scale_ref, o_ref, pe_ref, *, seq_len):
    i = pl.program_id(1)
    n = pl.num_programs(1)
    tl = p_ref.shape[0]
    t = POOL_CHUNK
    _fill_halo_buffer(pe_ref, p_ref, pprev_ref, pnext_ref, i == 0, i == n - 1)
    blocks = [(r0, g) for r0 in range(0, tl, t) for g in range(len(POOL_WINDOWS))]
    cols = lambda g: slice(g * POOL_GROUP_DIM, (g + 1) * POOL_GROUP_DIM)
    sums = [_dot(band_ref[g], pe_ref[r0:r0 + HALO_WINDOW, cols(g)]) for r0, g in blocks]
    pooled = []
    for (r0, g), acc in zip(blocks, sums):
        left = POOL_WINDOWS[g] // 2
        right = POOL_WINDOWS[g] - 1 - left
        pos = i * tl + r0 + lax.broadcasted_iota(jnp.int32, (t, 1), 0)
        cnt = (jnp.minimum(pos + right, seq_len - 1) - jnp.maximum(pos - left, 0) + 1).astype(F32)
        pooled.append((acc / cnt - pe_ref[HALO + r0:HALO + r0 + t, cols(g)].astype(F32)).astype(BF16))
    mixed = [_dot(pb, w_ref[g]) for (r0, g), pb in zip(blocks, pooled)]
    for (r0, g), y in zip(blocks, mixed):
        o_ref[r0:r0 + t, cols(g)] = (y * scale_ref[:, cols(g)]).astype(o_ref.dtype)


def _pool_mixer(pin, p, layer, tl):
    b, l, _ = pin.shape
    n_tiles = l // tl
    prev, nxt = _halo_specs(tl, BRANCH_WIDTH, n_tiles)
    band = jnp.asarray(np.stack([_band_matrix(range(-(w // 2), w - w // 2)) for w in POOL_WINDOWS]), BF16)
    return pl.pallas_call(
        functools.partial(_pool_kernel, seq_len=l),
        grid=(b, n_tiles),
        in_specs=[_tile_spec(tl, BRANCH_WIDTH), prev, nxt, _cspec(band), _wspec(p["pool_w"], layer),
                  _wspec(p["pool_scale"], layer)],
        out_specs=_tile_spec(tl, BRANCH_WIDTH),
        out_shape=jax.ShapeDtypeStruct((b, l, BRANCH_WIDTH), BF16),
        scratch_shapes=[pltpu.VMEM((tl + 2 * HALO, BRANCH_WIDTH), BF16)],
        compiler_params=_params(2),
        name="pool",
    )(pin, pin, pin, band, p["pool_w"], p["pool_scale"])


def _sgu_kernel(u_ref, v_ref, ng_ref, w_ref, bias_ref, o_ref):
    tl = u_ref.shape[0]
    gd = BRANCH_WIDTH // SGU_GROUPS
    for r in range(0, tl, SGU_CHUNK):
        rows = slice(r, r + SGU_CHUNK)
        vf = _rms(_gelu_tanh(v_ref[rows, :].astype(F32)), ng_ref[...]).astype(BF16)
        uf = _gelu_tanh(u_ref[rows, :].astype(F32))
        for g in range(SGU_GROUPS):
            cols = slice(g * gd, (g + 1) * gd)
            mixed = _dot(w_ref[g], vf[:, cols]) + bias_ref[:, cols]
            o_ref[rows, cols] = (uf[:, cols] * mixed).astype(o_ref.dtype)


def _sgu_mixer(u, v, p, layer, tl):
    b, l, _ = u.shape
    return pl.pallas_call(
        _sgu_kernel,
        grid=(b, l // tl),
        in_specs=[_tile_spec(tl, BRANCH_WIDTH), _tile_spec(tl, BRANCH_WIDTH), _wspec(p["sgu_norm"], layer),
                  _wspec(p["sgu_w"], layer), _wspec(p["sgu_bias"], layer)],
        out_specs=_tile_spec(tl, BRANCH_WIDTH),
        out_shape=jax.ShapeDtypeStruct((b, l, BRANCH_WIDTH), BF16),
        compiler_params=_params(2),
        name="sgu",
    )(u, v, p["sgu_norm"], p["sgu_w"], p["sgu_bias"])


def _gla_decay_matrix(ecol, col):
    return jnp.concatenate(
        [jnp.broadcast_to(ecol[h * GLA_KEY_DIM:(h + 1) * GLA_KEY_DIM, col:col + 1], (GLA_KEY_DIM, GLA_VAL_DIM))
         for h in range(GLA_HEADS)], axis=1)


def _gla_block_diag(compact):
    z = jnp.zeros((GLA_KEY_DIM, GLA_VAL_DIM), compact.dtype)
    return jnp.concatenate(
        [jnp.concatenate([compact[:, h * GLA_VAL_DIM:(h + 1) * GLA_VAL_DIM] if g == h else z
                          for g in range(GLA_HEADS)], axis=1) for h in range(GLA_HEADS)], axis=0)


def _gla_ecol(tots):
    padded = jnp.concatenate([tots, jnp.zeros((LANE - tots.shape[0], GLA_QK), F32)], axis=0)
    return jnp.exp(padded.T)


def _gla_local_block(blk, q_ref, k_ref, v_ref, glr_ref, w2_ref, gb_ref, tril_ref, cmask_ref, oi_ref, qd_ref,
                     tots_ref, results):
    t = GLA_CHUNK
    nch = GLA_TILE_CHUNKS
    tile = GLA_TILE
    rows = pl.ds(blk * tile, tile)

    pre = _dot(glr_ref[rows, :].astype(BF16), w2_ref[...]) + gb_ref[...]
    yield
    gk = -_softplus(-pre) * (1.0 / GLA_GATE_NORMALIZER)
    bc = _dot(tril_ref[...], gk.astype(BF16))
    yield
    last = [bc[c * t + t - 1:c * t + t, :] for c in range(nch)]
    tot = jnp.concatenate([jnp.broadcast_to(row, (t, 2 * GLA_QK)) for row in last], axis=0)
    bc_f = bc[:, :GLA_QK]
    bc_b = bc[:, GLA_QK:]
    rb = tot[:, GLA_QK:] - bc_b + gk[:, GLA_QK:]
    qf = q_ref[rows, :].astype(F32) * (GLA_KEY_DIM ** -0.5)
    kf = k_ref[rows, :].astype(F32)
    v = v_ref[rows, :]
    qd_f = qf * jnp.exp(bc_f)
    qd_b = qf * jnp.exp(rb)
    ki_f = (kf * jnp.exp(-bc_f)).astype(BF16)
    ki_b = (kf * jnp.exp(-rb)).astype(BF16)
    ke_f = kf * jnp.exp(tot[:, :GLA_QK] - bc_f)
    ke_b = kf * jnp.exp(bc_b - gk[:, GLA_QK:])
    qd_ref[rows, :GLA_QK] = qd_f.astype(qd_ref.dtype)
    qd_ref[rows, GLA_QK:] = qd_b.astype(qd_ref.dtype)

    lane = lax.broadcasted_iota(jnp.int32, (tile, GLA_QK), 1)

    def stack_heads(qd):
        return jnp.concatenate(
            [jnp.where((lane >= h * GLA_KEY_DIM) & (lane < (h + 1) * GLA_KEY_DIM), qd, 0.0)
             for h in range(GLA_HEADS)], axis=0).astype(BF16)

    att_f = _dot_nt(stack_heads(qd_f), ki_f)
    att_b = _dot_nt(stack_heads(qd_b), ki_b)
    yield
    row = lax.broadcasted_iota(jnp.int32, (tile, tile), 0)
    col = lax.broadcasted_iota(jnp.int32, (tile, tile), 1)
    same = (row & -t) == (col & -t)
    m_f = same & (row >= col)
    m_b = same & (row <= col)
    for h in range(GLA_HEADS):
        hrows = slice(h * tile, (h + 1) * tile)
        att = (jnp.where(m_f, att_f[hrows, :], 0.0) + jnp.where(m_b, att_b[hrows, :], 0.0)).astype(BF16)
        oi_ref[rows, h * GLA_VAL_DIM:(h + 1) * GLA_VAL_DIM] = _dot(
            att, v[:, h * GLA_VAL_DIM:(h + 1) * GLA_VAL_DIM]).astype(oi_ref.dtype)
    yield
    ket_f = ke_f.T.astype(BF16)
    ket_b = ke_b.T.astype(BF16)
    cmask = cmask_ref[...]
    st = []
    for h in range(GLA_HEADS):
        vh = v[:, h * GLA_VAL_DIM:(h + 1) * GLA_VAL_DIM]
        v_exp = jnp.concatenate([vh] * nch, axis=1) * cmask
        lhs = jnp.concatenate([ket_f[h * GLA_KEY_DIM:(h + 1) * GLA_KEY_DIM, :],
                               ket_b[h * GLA_KEY_DIM:(h + 1) * GLA_KEY_DIM, :]], axis=0)
        st.append(_dot(lhs, v_exp))
    tots = jnp.concatenate([r[:, :GLA_QK] for r in last] + [r[:, GLA_QK:] for r in last], axis=0)
    tots_ref[blk] = tots
    results.append((blk, _gla_ecol(tots), st))
    yield


def _run_staged(generators):
    live = list(generators)
    while live:
        live = [g for g in live if next(g, StopIteration) is not StopIteration]


def _gla_local_kernel(q_ref, k_ref, v_ref, glr_ref, w2_ref, gb_ref, tril_ref, cmask_ref,
                      oi_ref, qd_ref, sf_ref, tots_ref, sb_ref, s_ref):
    i = pl.program_id(1)
    nch = GLA_TILE_CHUNKS
    n_blk = q_ref.shape[0] // GLA_TILE

    @pl.when(i == 0)
    def _():
        s_ref[...] = jnp.zeros_like(s_ref)

    results = []
    _run_staged(_gla_local_block(blk, q_ref, k_ref, v_ref, glr_ref, w2_ref, gb_ref, tril_ref, cmask_ref, oi_ref,
                                 qd_ref, tots_ref, results) for blk in range(n_blk))

    for blk, ecol, st in sorted(results, key=lambda r: -r[0]):
        def local_state(c, lo):
            return jnp.concatenate([st[h][lo:lo + GLA_KEY_DIM, c * GLA_VAL_DIM:(c + 1) * GLA_VAL_DIM]
                                    for h in range(GLA_HEADS)], axis=1)

        for c in reversed(range(nch)):
            sf_ref[blk * nch + c] = local_state(c, 0).astype(sf_ref.dtype)
            sb_ref[blk * nch + c] = s_ref[...].astype(sb_ref.dtype)
            s_ref[...] = s_ref[...] * _gla_decay_matrix(ecol, nch + c) + local_state(c, GLA_KEY_DIM)


def _gla_final_kernel(oi_ref, qd_ref, sf_ref, tots_ref, sb_ref, r_ref, ng_ref, o_ref, s_ref):
    i = pl.program_id(1)
    t = GLA_CHUNK
    nch = GLA_TILE_CHUNKS
    n_blk = oi_ref.shape[0] // GLA_TILE

    @pl.when(i == 0)
    def _():
        s_ref[...] = jnp.zeros_like(s_ref)

    ng = ng_ref[...]
    ecol = [_gla_ecol(tots_ref[blk]) for blk in range(n_blk)]
    chunks = range(n_blk * nch)
    rows = [pl.ds(c * t, t) for c in chunks]
    s_f = []
    for c in chunks:
        s_f.append(s_ref[...].astype(BF16))
        s_ref[...] = s_ref[...] * _gla_decay_matrix(ecol[c // nch], c % nch) + sf_ref[c].astype(F32)
    inter_f = [_dot(qd_ref[rows[c], :GLA_QK], _gla_block_diag(s_f[c])) for c in chunks]
    inter_b = [_dot(qd_ref[rows[c], GLA_QK:], _gla_block_diag(sb_ref[c])) for c in chunks]
    for blk in range(n_blk):
        brows = pl.ds(blk * GLA_TILE, GLA_TILE)
        o = jnp.concatenate([oi_ref[rows[c], :].astype(F32) + inter_f[c] + inter_b[c]
                             for c in range(blk * nch, (blk + 1) * nch)], axis=0)
        rf = r_ref[brows, :].astype(F32)
        o = jnp.concatenate([_rms(o[:, h * GLA_VAL_DIM:(h + 1) * GLA_VAL_DIM], ng[:, h * GLA_VAL_DIM:(h + 1) * GLA_VAL_DIM])
                             for h in range(GLA_HEADS)], axis=1)
        o_ref[brows, :] = (o * (rf * _sigmoid(rf))).astype(o_ref.dtype)


def _gla_constants():
    r = np.arange(GLA_TILE)
    tril = ((r[:, None] >= r[None, :]) & (r[:, None] // GLA_CHUNK == r[None, :] // GLA_CHUNK)).astype(np.float32)
    ccol = np.arange(GLA_TILE_CHUNKS * GLA_VAL_DIM) // GLA_VAL_DIM
    cmask = (r[:, None] // GLA_CHUNK == ccol[None, :]).astype(np.float32)
    return jnp.asarray(tril, BF16), jnp.asarray(cmask, BF16)


def _gla_mixer(q, k, v, r, glr, p, layer, tl):
    b, l, _ = q.shape
    n_tiles = l // tl
    n_blk = tl // GLA_TILE
    nch = n_blk * GLA_TILE_CHUNKS
    nc = l // GLA_CHUNK
    tril, cmask = _gla_constants()
    oi, qd, sf, tots, sb = pl.pallas_call(
        _gla_local_kernel,
        grid=(b, n_tiles),
        in_specs=[_tile_spec(tl, GLA_QK, n_tiles), _tile_spec(tl, GLA_QK, n_tiles), _tile_spec(tl, BRANCH_WIDTH, n_tiles),
                  _tile_spec(tl, LANE, n_tiles), _wspec(p["gla_w2"], layer), _wspec(p["gla_gate_b"], layer),
                  _cspec(tril), _cspec(cmask)],
        out_specs=[_tile_spec(tl, BRANCH_WIDTH, n_tiles), _tile_spec(tl, 2 * GLA_QK, n_tiles),
                   _chunk_spec(nch, GLA_KEY_DIM, BRANCH_WIDTH, n_tiles), _chunk_spec(n_blk, 8, GLA_QK, n_tiles),
                   _chunk_spec(nch, GLA_KEY_DIM, BRANCH_WIDTH, n_tiles)],
        out_shape=[jax.ShapeDtypeStruct((b, l, BRANCH_WIDTH), BF16), jax.ShapeDtypeStruct((b, l, 2 * GLA_QK), BF16),
                   jax.ShapeDtypeStruct((b, nc, GLA_KEY_DIM, BRANCH_WIDTH), BF16),
                   jax.ShapeDtypeStruct((b, l // GLA_TILE, 8, GLA_QK), F32),
                   jax.ShapeDtypeStruct((b, nc, GLA_KEY_DIM, BRANCH_WIDTH), BF16)],
        scratch_shapes=[pltpu.VMEM((GLA_KEY_DIM, BRANCH_WIDTH), F32)],
        compiler_params=_params(2),
        name="gla_local",
    )(q, k, v, glr, p["gla_w2"], p["gla_gate_b"], tril, cmask)

    return pl.pallas_call(
        _gla_final_kernel,
        grid=(b, n_tiles),
        in_specs=[_tile_spec(tl, BRANCH_WIDTH), _tile_spec(tl, 2 * GLA_QK), _chunk_spec(nch, GLA_KEY_DIM, BRANCH_WIDTH),
                  _chunk_spec(n_blk, 8, GLA_QK), _chunk_spec(nch, GLA_KEY_DIM, BRANCH_WIDTH), _tile_spec(tl, BRANCH_WIDTH),
                  _wspec(p["gla_norm"], layer)],
        out_specs=_tile_spec(tl, BRANCH_WIDTH),
        out_shape=jax.ShapeDtypeStruct((b, l, BRANCH_WIDTH), BF16),
        scratch_shapes=[pltpu.VMEM((GLA_KEY_DIM, BRANCH_WIDTH), F32)],
        compiler_params=_params(2),
        name="gla_final",
    )(oi, qd, sf, tots, sb, r, p["gla_norm"])


def _merge_ffn_kernel(x_ref, b0_ref, b1_ref, b2_ref, b3_ref, gate_ref, wb_ref, wo_ref, gpost_ref, gpre_ref,
                      w1_ref, w2_ref, gffn_ref, o_ref):
    merged = None
    for n, b_ref in enumerate((b0_ref, b1_ref, b2_ref, b3_ref)):
        term = gate_ref[:, n * D_MODEL:(n + 1) * D_MODEL].astype(F32) * _dot(b_ref[...], wb_ref[n])
        merged = term if merged is None else merged + term
    x1 = x_ref[...] + _rms(_dot(merged.astype(BF16), wo_ref[...]), gpost_ref[...])
    h2 = _rms(x1, gpre_ref[...]).astype(BF16)
    ff = None
    for c in range(0, D_FF, D_MODEL):
        hid = jnp.square(jnp.maximum(_dot(h2, w1_ref[:, c:c + D_MODEL]), 0.0)).astype(BF16)
        part = _dot(hid, w2_ref[c:c + D_MODEL, :])
        ff = part if ff is None else ff + part
    o_ref[...] = x1 + _rms(ff, gffn_ref[...])


def _merge_ffn(x2d, branches, gate, p, layer, tm):
    n = x2d.shape[0]
    row = lambda w: pl.BlockSpec((tm, w), lambda i: (i, 0))
    return pl.pallas_call(
        _merge_ffn_kernel,
        grid=(n // tm,),
        in_specs=[row(D_MODEL)] + [row(BRANCH_WIDTH)] * N_BRANCH + [row(N_BRANCH * D_MODEL)] + [
            _wspec(p["w_branch"], layer), _wspec(p["w_out"], layer), _wspec(p["norm_mix_post"], layer),
            _wspec(p["norm_ffn_pre"], layer), _wspec(p["w_ff1"], layer), _wspec(p["w_ff2"], layer),
            _wspec(p["norm_ffn_post"], layer)],
        out_specs=row(D_MODEL),
        out_shape=jax.ShapeDtypeStruct((n, D_MODEL), F32),
        compiler_params=_params(1),
        name="merge_ffn",
    )(x2d, *branches, gate, p["w_branch"], p["w_out"], p["norm_mix_post"], p["norm_ffn_pre"], p["w_ff1"],
      p["w_ff2"], p["norm_ffn_post"])


def _pad_lanes(a, width):
    return jnp.pad(a, [(0, 0)] * (a.ndim - 1) + [(0, width - a.shape[-1])])


def _prepare_params(norm_mix_pre, w_in, ssd_conv_w, ssd_conv_b, ssd_a_log, ssd_dt_bias, ssd_d, ssd_norm, pool_w,
                    pool_scale, sgu_norm, sgu_w, sgu_b, gla_gate_w2, gla_gate_b, gla_norm, w_branch, w_out,
                    norm_mix_post, norm_ffn_pre, w_ff1, w_ff2, norm_ffn_post):
    depth = w_in.shape[0]
    pieces, off = [], 0
    for _, width, padded, _ in PROJ_GROUPS:
        pieces.append(_pad_lanes(w_in[:, :, off:off + width], padded))
        off += width
    row = lambda a: a.reshape(depth, 1, -1).astype(F32)
    w2 = jnp.zeros((depth, LANE, 2 * GLA_QK), F32)
    w2 = w2.at[:, :GLA_GATE_RANK, :GLA_QK].set(gla_gate_w2[:, 0])
    w2 = w2.at[:, GLA_GATE_RANK:2 * GLA_GATE_RANK, GLA_QK:].set(gla_gate_w2[:, 1])
    return {
        "norm_mix_pre": row(norm_mix_pre),
        "w_in": jnp.concatenate(pieces, axis=-1).astype(BF16),
        "ssd_conv_w": jnp.pad(ssd_conv_w.astype(F32), [(0, 0), (0, 8 - SSD_CONV), (0, 0)]),
        "ssd_conv_b": row(ssd_conv_b),
        "ssd_a_log": _pad_lanes(row(ssd_a_log), LANE),
        "ssd_dt_bias": _pad_lanes(row(ssd_dt_bias), LANE),
        "ssd_d": row(jnp.repeat(ssd_d, SSD_HEAD_DIM, axis=-1)),
        "ssd_norm": row(ssd_norm),
        "pool_w": pool_w.astype(BF16),
        "pool_scale": row(pool_scale),
        "sgu_norm": row(sgu_norm),
        "sgu_w": sgu_w.astype(BF16),
        "sgu_bias": jnp.repeat(jnp.swapaxes(sgu_b, 1, 2), BRANCH_WIDTH // SGU_GROUPS, axis=-1).astype(F32),
        "gla_w2": w2.astype(BF16),
        "gla_gate_b": row(gla_gate_b),
        "gla_norm": row(gla_norm),
        "w_branch": w_branch.astype(BF16),
        "w_out": w_out.astype(BF16),
        "norm_mix_post": row(norm_mix_post),
        "norm_ffn_pre": row(norm_ffn_pre),
        "w_ff1": w_ff1.astype(BF16),
        "w_ff2": w_ff2.astype(BF16),
        "norm_ffn_post": row(norm_ffn_post),
    }


def _tiles(seq_len):
    return min(512, seq_len), min(512, seq_len), min(512, seq_len)


def _trunk(x, p):
    b, l, d = x.shape
    tm_proj, tm_ffn, tl = _tiles(l)
    x2d = x.reshape(b * l, d)
    for layer in range(DEPTH):
        proj = dict(zip([g[0] for g in PROJ_GROUPS], _in_proj(x2d, p["norm_mix_pre"], p["w_in"], layer, tm_proj)))
        seq = lambda name: proj[name].reshape(b, l, -1)
        branches = (
            _ssd_mixer(seq("z"), seq("xbc"), seq("dt"), p, layer, tl),
            _pool_mixer(seq("pool"), p, layer, tl),
            _sgu_mixer(seq("u"), seq("v"), p, layer, tl),
            _gla_mixer(seq("q"), seq("k"), seq("gv"), seq("r"), seq("glr"), p, layer, tl),
        )
        branches = [br.reshape(b * l, BRANCH_WIDTH) for br in branches]
        x2d = _merge_ffn(x2d, branches, proj["gate"], p, layer, tm_ffn)
    return x2d.reshape(b, l, d)


def kernel(x_prompt, x_sample, norm_mix_pre, w_in, ssd_conv_w, ssd_conv_b, ssd_a_log, ssd_dt_bias, ssd_d, ssd_norm, pool_w, pool_scale, sgu_norm, sgu_w, sgu_b, gla_gate_w2, gla_gate_b, gla_norm, w_branch, w_out, norm_mix_post, norm_ffn_pre, w_ff1, w_ff2, norm_ffn_post):
    p = _prepare_params(norm_mix_pre, w_in, ssd_conv_w, ssd_conv_b, ssd_a_log, ssd_dt_bias, ssd_d, ssd_norm, pool_w,
                        pool_scale, sgu_norm, sgu_w, sgu_b, gla_gate_w2, gla_gate_b, gla_norm, w_branch, w_out,
                        norm_mix_post, norm_ffn_pre, w_ff1, w_ff2, norm_ffn_post)
    return (_trunk(x_prompt, p), _trunk(x_sample, p))
```

```python
import functools

import jax
import jax.numpy as jnp
import numpy as np
from jax import lax
from jax.experimental import pallas as pl
from jax.experimental.pallas import tpu as pltpu

F32 = jnp.float32
BF16 = jnp.bfloat16

D_MODEL = 1024
DEPTH = 4
BRANCH_WIDTH = 512
N_BRANCH = 4
SSD_HEADS = 8
SSD_HEAD_DIM = 64
SSD_GROUPS = 2
SSD_STATE = 64
SSD_CONV = 5
SSD_CHUNK = 128
SSD_CONV_CH = 768
POOL_WINDOWS = (2, 4, 8, 16)
POOL_GROUP_DIM = 128
POOL_CHUNK = 128
SGU_CHUNK = 128
SGU_GROUPS = 4
GLA_HEADS = 4
GLA_KEY_DIM = 64
GLA_VAL_DIM = 128
GLA_GATE_RANK = 16
GLA_GATE_NORMALIZER = 16.0
GLA_CHUNK = 64
GLA_QK = GLA_HEADS * GLA_KEY_DIM
GLA_TILE = 256
GLA_TILE_CHUNKS = GLA_TILE // GLA_CHUNK
D_FF = 4096
RMS_EPS = 1e-6

LANE = 128
HALO = 16
VMEM_LIMIT = 56 * 1024 * 1024

PROJ_GROUPS = (
    ("z", 512, 512, BF16), ("xbc", 768, 768, BF16), ("dt", 16, LANE, F32), ("pool", 512, 512, BF16),
    ("u", 512, 512, BF16), ("v", 512, 512, BF16), ("q", 256, 256, BF16), ("k", 256, 256, BF16),
    ("gv", 512, 512, BF16), ("r", 512, 512, BF16), ("glr", 32, LANE, F32), ("gate", 4096, 4096, BF16),
)
N_IN_PAD = sum(g[2] for g in PROJ_GROUPS)


LOG2_E = float(np.log2(np.e))


def _sigmoid(x):
    return 1.0 / (1.0 + jnp.exp2(x * (-LOG2_E)))


def _softplus(x):
    return jnp.maximum(x, 0.0) + jnp.log(1.0 + jnp.exp(-jnp.abs(x)))


def _gelu_tanh(x):
    c = -2.0 * np.sqrt(2.0 / np.pi) * LOG2_E
    return x / (1.0 + jnp.exp2(x * (c + (c * 0.044715) * (x * x))))


def _rms(x, g):
    return x * lax.rsqrt(jnp.mean(x * x, axis=-1, keepdims=True) + RMS_EPS) * g


def _dot(a, b):
    return jnp.dot(a, b, preferred_element_type=F32)


def _dot_tn(a, b):
    return lax.dot_general(a, b, (((0,), (0,)), ((), ())), preferred_element_type=F32)


def _dot_nt(a, b):
    return lax.dot_general(a, b, (((1,), (1,)), ((), ())), preferred_element_type=F32)


def _split_bf16(x):
    hi = x.astype(BF16)
    lo = (x - hi.astype(F32)).astype(BF16)
    return hi, lo


def _dot_split_rhs(a_bf16, x):
    hi, lo = _split_bf16(x)
    return _dot(a_bf16, hi) + _dot(a_bf16, lo)


def _dot_split_lhs(x, b_bf16):
    hi, lo = _split_bf16(x)
    return _dot(hi, b_bf16) + _dot(lo, b_bf16)


def _wspec(arr, layer):
    nd = arr.ndim - 1
    return pl.BlockSpec((None,) + tuple(arr.shape[1:]), lambda *_: (layer,) + (0,) * nd,
                        pipeline_mode=pl.Buffered(1))


def _cspec(arr):
    nd = arr.ndim
    return pl.BlockSpec(tuple(arr.shape), lambda *_: (0,) * nd, pipeline_mode=pl.Buffered(1))


def _params(n_grid):
    return pltpu.CompilerParams(dimension_semantics=("arbitrary",) * n_grid, vmem_limit_bytes=VMEM_LIMIT)


def _run_staged(generators):
    live = list(generators)
    while live:
        live = [g for g in live if next(g, StopIteration) is not StopIteration]


def _in_proj_kernel(x_ref, g_ref, w_ref, *out_refs):
    h = _rms(x_ref[...], g_ref[...]).astype(BF16)
    off = 0
    for o_ref, (name, _, width, dtype) in zip(out_refs, PROJ_GROUPS):
        for c in range(0, width, 512):
            cw = min(512, width - c)
            acc = _dot(h, w_ref[:, off + c:off + c + cw])
            if name == "gate":
                acc = _sigmoid(acc)
            o_ref[:, c:c + cw] = acc.astype(dtype)
        off += width


def _in_proj(x2d, norm_g, w_in_p, layer, tm):
    n = x2d.shape[0]
    out_shape = [jax.ShapeDtypeStruct((n, g[2]), g[3]) for g in PROJ_GROUPS]
    out_specs = [pl.BlockSpec((tm, g[2]), lambda i: (i, 0)) for g in PROJ_GROUPS]
    return pl.pallas_call(
        _in_proj_kernel,
        grid=(n // tm,),
        in_specs=[pl.BlockSpec((tm, D_MODEL), lambda i: (i, 0)), _wspec(norm_g, layer), _wspec(w_in_p, layer)],
        out_specs=out_specs,
        out_shape=out_shape,
        compiler_params=_params(1),
        name="in_proj",
    )(x2d, norm_g, w_in_p)


def _tile_spec(tl, width, n_tiles=None):
    if n_tiles is None:
        return pl.BlockSpec((None, tl, width), lambda b, i: (b, i, 0))
    return pl.BlockSpec((None, tl, width), lambda b, i: (b, n_tiles - 1 - i, 0))


def _chunk_spec(per_tile, rows, width, n_tiles=None):
    if n_tiles is None:
        return pl.BlockSpec((None, per_tile, rows, width), lambda b, i: (b, i, 0, 0))
    return pl.BlockSpec((None, per_tile, rows, width), lambda b, i: (b, n_tiles - 1 - i, 0, 0))


def _halo_specs(tl, width, n_tiles, reverse=False):
    per = tl // HALO
    last = n_tiles * per - 1

    def tile(i):
        return n_tiles - 1 - i if reverse else i

    prev = pl.BlockSpec((None, HALO, width), lambda b, i: (b, jnp.maximum(tile(i) * per - 1, 0), 0))
    nxt = pl.BlockSpec((None, HALO, width), lambda b, i: (b, jnp.minimum((tile(i) + 1) * per, last), 0))
    return prev, nxt


def _fill_halo_buffer(ext_ref, cur_ref, prev_ref, next_ref, is_first, is_last):
    tl = cur_ref.shape[0]
    zeros = jnp.zeros((HALO, ext_ref.shape[1]), ext_ref.dtype)
    ext_ref[0:HALO, :] = jnp.where(is_first, zeros, prev_ref[...])
    ext_ref[HALO:HALO + tl, :] = cur_ref[...]
    ext_ref[HALO + tl:2 * HALO + tl, :] = jnp.where(is_last, zeros, next_ref[...])


HALO_WINDOW = 128 + 2 * HALO


def _band_matrix(offsets):
    r = np.arange(128)[:, None]
    w = np.arange(HALO_WINDOW)[None, :]
    return sum((w == r + HALO + off).astype(np.float32) for off in offsets)


def _ssd_conv(xe_ref, r0, shift_ref, cw_ref, cb_ref):
    t = SSD_CHUNK
    pad = SSD_CONV // 2
    win = xe_ref[pl.ds(r0, HALO_WINDOW), :]
    shifted = _dot(shift_ref[...], win)
    acc = cb_ref[...] + cw_ref[pad:pad + 1, :] * win[HALO:HALO + t, :].astype(F32)
    for n, j in enumerate(j for j in range(SSD_CONV) if j != pad):
        acc = acc + cw_ref[j:j + 1, :] * shifted[n * t:(n + 1) * t, :]
    return acc * _sigmoid(acc)


def _ssd_decay_terms(dt_raw, dtb_ref, alog_ref, tril_ref, triu_ref):
    t = dt_raw.shape[0]
    lane = lax.broadcasted_iota(jnp.int32, (t, LANE), 1)
    used = lane < 2 * SSD_HEADS
    dt = jnp.where(used, _softplus(dt_raw + dtb_ref[...]), 0.0)
    a = jnp.where(used[0:1], -jnp.exp(alog_ref[...]), 0.0)
    dta = dt * a
    cf = _dot_split_rhs(tril_ref[...], dta)
    cr = _dot_split_rhs(triu_ref[...], dta)
    fwd = lane < SSD_HEADS
    acs = jnp.where(fwd, cf, cr)
    tot = jnp.where(fwd[0:1], cf[t - 1:t, :], cr[0:1, :])
    return dt, acs, tot


def _ssd_compact(full):
    half = BRANCH_WIDTH // SSD_GROUPS
    return jnp.concatenate([full[:SSD_STATE, :half], full[SSD_STATE:, half:]], axis=1)


def _ssd_expand(compact):
    half = BRANCH_WIDTH // SSD_GROUPS
    z = jnp.zeros((SSD_STATE, half), compact.dtype)
    return jnp.concatenate([jnp.concatenate([compact[:, :half], z], axis=1),
                            jnp.concatenate([z, compact[:, half:]], axis=1)], axis=0)


def _ssd_local_stages(xbc_ref, xprev_ref, xnext_ref, dt_ref, cw_ref, cb_ref, alog_ref, dtb_ref, dexp_ref,
                      tril_ref, triu_ref, ef_ref, eb_ref, shift_ref,
                      yp_ref, cs_ref, e_ref, sf_ref, dec_ref, sb_ref, xe_ref, s_ref):
    i = pl.program_id(1)
    n = pl.num_programs(1)
    nci = xbc_ref.shape[0] // SSD_CHUNK
    t = SSD_CHUNK

    @pl.when(i == 0)
    def _():
        s_ref[...] = jnp.zeros_like(s_ref)

    _fill_halo_buffer(xe_ref, xbc_ref, xprev_ref, xnext_ref, i == n - 1, i == 0)

    lane = lax.broadcasted_iota(jnp.int32, (t, LANE), 1)
    low = lane < SSD_STATE
    row = lax.broadcasted_iota(jnp.int32, (t, t), 0)
    col = lax.broadcasted_iota(jnp.int32, (t, t), 1)
    tge = row >= col
    tle = row <= col
    hpg = SSD_HEADS // SSD_GROUPS

    chunks = range(nci)
    xc = []
    for c in chunks:
        xc.append(_ssd_conv(xe_ref, c * t, shift_ref, cw_ref, cb_ref))
        yield
    xs = [a[:, :BRANCH_WIDTH] for a in xc]
    bs = [a[:, BRANCH_WIDTH:BRANCH_WIDTH + LANE].astype(BF16) for a in xc]
    cs = [a[:, BRANCH_WIDTH + LANE:] for a in xc]
    terms = [_ssd_decay_terms(dt_ref[pl.ds(c * t, t), :], dtb_ref, alog_ref, tril_ref, triu_ref) for c in chunks]
    dt = [a[0] for a in terms]
    acs = [a[1] for a in terms]
    tot = [a[2] for a in terms]
    yield
    cb = [(_dot_nt(jnp.where(low, cs[c], 0.0).astype(BF16), bs[c]),
           _dot_nt(jnp.where(low, 0.0, cs[c]).astype(BF16), bs[c])) for c in chunks]
    acs_t = [a.T for a in acs]
    dt_t = [a.T for a in dt]
    yield
    w = [(jnp.exp(tot[c] - acs[c]) * dt[c]).astype(BF16) for c in chunks]
    w_f = [_dot(w[c], ef_ref[...]) for c in chunks]
    w_b = [_dot(w[c], eb_ref[...]) for c in chunks]
    dec8 = [jnp.broadcast_to(jnp.exp(tot[c]), (8, LANE)) for c in chunks]
    dec_f = [_dot_split_lhs(dec8[c], ef_ref[...]) for c in chunks]
    dec_b = [_dot_split_lhs(dec8[c], eb_ref[...]) for c in chunks]
    yield

    y_pairs = [[] for _ in chunks]
    for pair in range(SSD_HEADS // 2):
        acc = [None for _ in chunks]
        for sub in range(2):
            h = 2 * pair + sub
            hb = SSD_HEADS + h
            half_mask = low if sub == 0 else jnp.logical_not(low)
            for c in chunks:
                arg = jnp.where(tge, acs[c][:, h:h + 1] - acs_t[c][h:h + 1, :],
                                acs[c][:, hb:hb + 1] - acs_t[c][hb:hb + 1, :])
                wgt = jnp.where(tge, dt_t[c][h:h + 1, :], 0.0) + jnp.where(tle, dt_t[c][hb:hb + 1, :], 0.0)
                m = (cb[c][h // hpg] * jnp.exp(arg) * wgt).astype(BF16)
                x_half = jnp.where(half_mask, xs[c][:, pair * LANE:(pair + 1) * LANE], 0.0).astype(BF16)
                part = _dot(m, x_half)
                acc[c] = part if acc[c] is None else acc[c] + part
            yield
        for c in chunks:
            y_pairs[c].append(acc[c])

    st_f = [_ssd_compact(_dot_tn(bs[c], (xs[c] * w_f[c]).astype(BF16))) for c in chunks]
    st_b = [_ssd_compact(_dot_tn(bs[c], (xs[c] * w_b[c]).astype(BF16))) for c in chunks]
    yield
    srow = lax.broadcasted_iota(jnp.int32, (8, BRANCH_WIDTH), 0)
    for c in chunks:
        rows = pl.ds(c * t, t)
        y = jnp.concatenate(y_pairs[c], axis=1) + xs[c] * dexp_ref[...]
        yp_ref[rows, :] = y.astype(yp_ref.dtype)
        cs_ref[rows, :] = cs[c].astype(cs_ref.dtype)
        e_ref[rows, :] = jnp.exp(acs[c]).astype(e_ref.dtype)
        dec_ref[c] = jnp.where(srow == 0, dec_f[c], dec_b[c])
        sf_ref[c] = st_f[c].astype(sf_ref.dtype)
        yield
    for c in reversed(chunks):
        sb_ref[c] = s_ref[...].astype(sb_ref.dtype)
        s_ref[...] = s_ref[...] * dec_b[c][0:1, :] + st_b[c]


def _ssd_final_stages(yp_ref, cs_ref, e_ref, z_ref, sf_ref, dec_ref, sb_ref, ng_ref, ef_ref, eb_ref, o_ref, s_ref):
    i = pl.program_id(1)
    nci = yp_ref.shape[0] // SSD_CHUNK
    t = SSD_CHUNK

    @pl.when(i == 0)
    def _():
        s_ref[...] = jnp.zeros_like(s_ref)

    half = BRANCH_WIDTH // SSD_GROUPS
    ng = ng_ref[...]
    chunks = range(nci)
    rows = [pl.ds(c * t, t) for c in chunks]
    s_f = []
    for c in chunks:
        s_f.append(s_ref[...].astype(BF16))
        s_ref[...] = s_ref[...] * dec_ref[c][0:1, :] + sf_ref[c].astype(F32)
    yield
    ex_f = [_dot(e_ref[rows[c], :], ef_ref[...]) for c in chunks]
    ex_b = [_dot(e_ref[rows[c], :], eb_ref[...]) for c in chunks]
    yield
    off_f = [_dot(cs_ref[rows[c], :], _ssd_expand(s_f[c])) for c in chunks]
    off_b = [_dot(cs_ref[rows[c], :], _ssd_expand(sb_ref[c])) for c in chunks]
    yield
    for c in chunks:
        y = yp_ref[rows[c], :].astype(F32) + ex_f[c] * off_f[c] + ex_b[c] * off_b[c]
        zf = z_ref[rows[c], :].astype(F32)
        y = y * (zf * _sigmoid(zf))
        y = jnp.concatenate([_rms(y[:, g * half:(g + 1) * half], ng[:, g * half:(g + 1) * half])
                             for g in range(SSD_GROUPS)], axis=1)
        o_ref[rows[c], :] = y.astype(o_ref.dtype)
        yield


def _ssd_constants():
    t = SSD_CHUNK
    r = np.arange(t)
    tril = (r[:, None] >= r[None, :]).astype(np.float32)
    triu = (r[:, None] <= r[None, :]).astype(np.float32)
    col = np.arange(BRANCH_WIDTH)
    ef = (r[:, None] == (col[None, :] // SSD_HEAD_DIM)).astype(np.float32)
    eb = (r[:, None] == (col[None, :] // SSD_HEAD_DIM) + SSD_HEADS).astype(np.float32)
    pad = SSD_CONV // 2
    shift = np.concatenate([_band_matrix([j - pad]) for j in range(SSD_CONV) if j != pad], axis=0)
    return [jnp.asarray(a, BF16) for a in (tril, triu, ef, eb, shift)]


def _pool_stages(p_ref, pprev_ref, pnext_ref, band_ref, w_ref, scale_ref, o_ref, pe_ref, seq_len):
    i = pl.program_id(1)
    n = pl.num_programs(1)
    tl = p_ref.shape[0]
    t = POOL_CHUNK
    _fill_halo_buffer(pe_ref, p_ref, pprev_ref, pnext_ref, i == n - 1, i == 0)
    tile_start = (n - 1 - i) * tl
    blocks = [(r0, g) for r0 in range(0, tl, t) for g in range(len(POOL_WINDOWS))]
    cols = lambda g: slice(g * POOL_GROUP_DIM, (g + 1) * POOL_GROUP_DIM)
    sums = []
    for r0, g in blocks:
        sums.append(_dot(band_ref[g], pe_ref[r0:r0 + HALO_WINDOW, cols(g)]))
        if g == len(POOL_WINDOWS) - 1:
            yield
    pooled = []
    for (r0, g), acc in zip(blocks, sums):
        left = POOL_WINDOWS[g] // 2
        right = POOL_WINDOWS[g] - 1 - left
        pos = tile_start + r0 + lax.broadcasted_iota(jnp.int32, (t, 1), 0)
        cnt = (jnp.minimum(pos + right, seq_len - 1) - jnp.maximum(pos - left, 0) + 1).astype(F32)
        pooled.append((acc / cnt - pe_ref[HALO + r0:HALO + r0 + t, cols(g)].astype(F32)).astype(BF16))
        if g == len(POOL_WINDOWS) - 1:
            yield
    mixed = []
    for (r0, g), pb in zip(blocks, pooled):
        mixed.append(_dot(pb, w_ref[g]))
        if g == len(POOL_WINDOWS) - 1:
            yield
    for (r0, g), y in zip(blocks, mixed):
        o_ref[r0:r0 + t, cols(g)] = (y * scale_ref[:, cols(g)]).astype(o_ref.dtype)
    yield


def _gla_decay_matrix(ecol, col):
    return jnp.concatenate(
        [jnp.broadcast_to(ecol[h * GLA_KEY_DIM:(h + 1) * GLA_KEY_DIM, col:col + 1], (GLA_KEY_DIM, GLA_VAL_DIM))
         for h in range(GLA_HEADS)], axis=1)


def _gla_block_diag(compact):
    z = jnp.zeros((GLA_KEY_DIM, GLA_VAL_DIM), compact.dtype)
    return jnp.concatenate(
        [jnp.concatenate([compact[:, h * GLA_VAL_DIM:(h + 1) * GLA_VAL_DIM] if g == h else z
                          for g in range(GLA_HEADS)], axis=1) for h in range(GLA_HEADS)], axis=0)


def _gla_ecol(tots):
    padded = jnp.concatenate([tots, jnp.zeros((LANE - tots.shape[0], GLA_QK), F32)], axis=0)
    return jnp.exp(padded.T)


def _gla_local_block(blk, q_ref, k_ref, v_ref, glr_ref, w2_ref, gb_ref, tril_ref, cmask_ref, oi_ref, qd_ref,
                     tots_ref, results):
    t = GLA_CHUNK
    nch = GLA_TILE_CHUNKS
    tile = GLA_TILE
    rows = pl.ds(blk * tile, tile)

    pre = _dot(glr_ref[rows, :].astype(BF16), w2_ref[...]) + gb_ref[...]
    yield
    gk = -_softplus(-pre) * (1.0 / GLA_GATE_NORMALIZER)
    bc = _dot(tril_ref[...], gk.astype(BF16))
    yield
    last = [bc[c * t + t - 1:c * t + t, :] for c in range(nch)]
    tot = jnp.concatenate([jnp.broadcast_to(row, (t, 2 * GLA_QK)) for row in last], axis=0)
    bc_f = bc[:, :GLA_QK]
    bc_b = bc[:, GLA_QK:]
    rb = tot[:, GLA_QK:] - bc_b + gk[:, GLA_QK:]
    qf = q_ref[rows, :].astype(F32) * (GLA_KEY_DIM ** -0.5)
    kf = k_ref[rows, :].astype(F32)
    v = v_ref[rows, :]
    qd_f = qf * jnp.exp(bc_f)
    qd_b = qf * jnp.exp(rb)
    ki_f = (kf * jnp.exp(-bc_f)).astype(BF16)
    ki_b = (kf * jnp.exp(-rb)).astype(BF16)
    ke_f = kf * jnp.exp(tot[:, :GLA_QK] - bc_f)
    ke_b = kf * jnp.exp(bc_b - gk[:, GLA_QK:])
    qd_ref[rows, :GLA_QK] = qd_f.astype(qd_ref.dtype)
    qd_ref[rows, GLA_QK:] = qd_b.astype(qd_ref.dtype)

    lane = lax.broadcasted_iota(jnp.int32, (tile, GLA_QK), 1)

    def stack_heads(qd):
        return jnp.concatenate(
            [jnp.where((lane >= h * GLA_KEY_DIM) & (lane < (h + 1) * GLA_KEY_DIM), qd, 0.0)
             for h in range(GLA_HEADS)], axis=0).astype(BF16)

    att_f = _dot_nt(stack_heads(qd_f), ki_f)
    att_b = _dot_nt(stack_heads(qd_b), ki_b)
    yield
    row = lax.broadcasted_iota(jnp.int32, (tile, tile), 0)
    col = lax.broadcasted_iota(jnp.int32, (tile, tile), 1)
    same = (row & -t) == (col & -t)
    m_f = same & (row >= col)
    m_b = same & (row <= col)
    for h in range(GLA_HEADS):
        hrows = slice(h * tile, (h + 1) * tile)
        att = (jnp.where(m_f, att_f[hrows, :], 0.0) + jnp.where(m_b, att_b[hrows, :], 0.0)).astype(BF16)
        oi_ref[rows, h * GLA_VAL_DIM:(h + 1) * GLA_VAL_DIM] = _dot(
            att, v[:, h * GLA_VAL_DIM:(h + 1) * GLA_VAL_DIM]).astype(oi_ref.dtype)
    yield
    ket_f = ke_f.T.astype(BF16)
    ket_b = ke_b.T.astype(BF16)
    cmask = cmask_ref[...]
    st = []
    for h in range(GLA_HEADS):
        vh = v[:, h * GLA_VAL_DIM:(h + 1) * GLA_VAL_DIM]
        v_exp = jnp.concatenate([vh] * nch, axis=1) * cmask
        lhs = jnp.concatenate([ket_f[h * GLA_KEY_DIM:(h + 1) * GLA_KEY_DIM, :],
                               ket_b[h * GLA_KEY_DIM:(h + 1) * GLA_KEY_DIM, :]], axis=0)
        st.append(_dot(lhs, v_exp))
    tots = jnp.concatenate([r[:, :GLA_QK] for r in last] + [r[:, GLA_QK:] for r in last], axis=0)
    tots_ref[blk] = tots
    results.append((blk, _gla_ecol(tots), st))
    yield


def _gla_local_stages(q_ref, k_ref, v_ref, glr_ref, w2_ref, gb_ref, tril_ref, cmask_ref,
                      oi_ref, qd_ref, sf_ref, tots_ref, sb_ref, s_ref):
    i = pl.program_id(1)
    nch = GLA_TILE_CHUNKS
    n_blk = q_ref.shape[0] // GLA_TILE

    @pl.when(i == 0)
    def _():
        s_ref[...] = jnp.zeros_like(s_ref)

    results = []
    live = [_gla_local_block(blk, q_ref, k_ref, v_ref, glr_ref, w2_ref, gb_ref, tril_ref, cmask_ref, oi_ref,
                             qd_ref, tots_ref, results) for blk in range(n_blk)]
    while live:
        live = [g for g in live if next(g, StopIteration) is not StopIteration]
        yield

    for blk, ecol, st in sorted(results, key=lambda r: -r[0]):
        def local_state(c, lo):
            return jnp.concatenate([st[h][lo:lo + GLA_KEY_DIM, c * GLA_VAL_DIM:(c + 1) * GLA_VAL_DIM]
                                    for h in range(GLA_HEADS)], axis=1)

        for c in reversed(range(nch)):
            sf_ref[blk * nch + c] = local_state(c, 0).astype(sf_ref.dtype)
            sb_ref[blk * nch + c] = s_ref[...].astype(sb_ref.dtype)
            s_ref[...] = s_ref[...] * _gla_decay_matrix(ecol, nch + c) + local_state(c, GLA_KEY_DIM)
        yield


def _gla_final_stages(oi_ref, qd_ref, sf_ref, tots_ref, sb_ref, r_ref, ng_ref, o_ref, s_ref):
    i = pl.program_id(1)
    t = GLA_CHUNK
    nch = GLA_TILE_CHUNKS
    n_blk = oi_ref.shape[0] // GLA_TILE

    @pl.when(i == 0)
    def _():
        s_ref[...] = jnp.zeros_like(s_ref)

    ng = ng_ref[...]
    ecol = [_gla_ecol(tots_ref[blk]) for blk in range(n_blk)]
    chunks = range(n_blk * nch)
    rows = [pl.ds(c * t, t) for c in chunks]
    s_f = []
    for c in chunks:
        s_f.append(s_ref[...].astype(BF16))
        s_ref[...] = s_ref[...] * _gla_decay_matrix(ecol[c // nch], c % nch) + sf_ref[c].astype(F32)
    yield
    inter_f = [_dot(qd_ref[rows[c], :GLA_QK], _gla_block_diag(s_f[c])) for c in chunks]
    yield
    inter_b = [_dot(qd_ref[rows[c], GLA_QK:], _gla_block_diag(sb_ref[c])) for c in chunks]
    yield
    for blk in range(n_blk):
        brows = pl.ds(blk * GLA_TILE, GLA_TILE)
        o = jnp.concatenate([oi_ref[rows[c], :].astype(F32) + inter_f[c] + inter_b[c]
                             for c in range(blk * nch, (blk + 1) * nch)], axis=0)
        rf = r_ref[brows, :].astype(F32)
        o = jnp.concatenate([_rms(o[:, h * GLA_VAL_DIM:(h + 1) * GLA_VAL_DIM], ng[:, h * GLA_VAL_DIM:(h + 1) * GLA_VAL_DIM])
                             for h in range(GLA_HEADS)], axis=1)
        o_ref[brows, :] = (o * (rf * _sigmoid(rf))).astype(o_ref.dtype)
        yield


def _gla_constants():
    r = np.arange(GLA_TILE)
    tril = ((r[:, None] >= r[None, :]) & (r[:, None] // GLA_CHUNK == r[None, :] // GLA_CHUNK)).astype(np.float32)
    ccol = np.arange(GLA_TILE_CHUNKS * GLA_VAL_DIM) // GLA_VAL_DIM
    cmask = (r[:, None] // GLA_CHUNK == ccol[None, :]).astype(np.float32)
    return jnp.asarray(tril, BF16), jnp.asarray(cmask, BF16)


def _local_kernel(xbc_ref, xprev_ref, xnext_ref, dt_ref, q_ref, k_ref, v_ref, glr_ref, pin_ref, pprev_ref, pnext_ref,
                  cw_ref, cb_ref, alog_ref, dtb_ref, dexp_ref, w2_ref, gb_ref, pw_ref, pscale_ref,
                  stril_ref, striu_ref, ef_ref, eb_ref, shift_ref, gtril_ref, cmask_ref, band_ref,
                  yp_ref, cs_ref, e_ref, ssf_ref, dec_ref, ssb_ref, oi_ref, qd_ref, gsf_ref, tots_ref, gsb_ref,
                  pout_ref, xe_ref, pe_ref, s_ssd_ref, s_gla_ref, *, seq_len):
    _run_staged([
        _ssd_local_stages(xbc_ref, xprev_ref, xnext_ref, dt_ref, cw_ref, cb_ref, alog_ref, dtb_ref, dexp_ref,
                          stril_ref, striu_ref, ef_ref, eb_ref, shift_ref,
                          yp_ref, cs_ref, e_ref, ssf_ref, dec_ref, ssb_ref, xe_ref, s_ssd_ref),
        _gla_local_stages(q_ref, k_ref, v_ref, glr_ref, w2_ref, gb_ref, gtril_ref, cmask_ref,
                          oi_ref, qd_ref, gsf_ref, tots_ref, gsb_ref, s_gla_ref),
        _pool_stages(pin_ref, pprev_ref, pnext_ref, band_ref, pw_ref, pscale_ref, pout_ref, pe_ref, seq_len),
    ])


def _final_kernel(yp_ref, cs_ref, e_ref, z_ref, ssf_ref, dec_ref, ssb_ref, oi_ref, qd_ref, gsf_ref, tots_ref, gsb_ref,
                  r_ref, sng_ref, gng_ref, ef_ref, eb_ref, ssd_out_ref, gla_out_ref, s_ssd_ref, s_gla_ref):
    _run_staged([
        _ssd_final_stages(yp_ref, cs_ref, e_ref, z_ref, ssf_ref, dec_ref, ssb_ref, sng_ref, ef_ref, eb_ref,
                          ssd_out_ref, s_ssd_ref),
        _gla_final_stages(oi_ref, qd_ref, gsf_ref, tots_ref, gsb_ref, r_ref, gng_ref, gla_out_ref, s_gla_ref),
    ])


def _mixers(proj, p, layer, b, l, tl):
    seq = lambda name: proj[name].reshape(b, l, -1)
    n_tiles = l // tl
    nci = tl // SSD_CHUNK
    n_blk = tl // GLA_TILE
    nch = n_blk * GLA_TILE_CHUNKS
    stril, striu, ef, eb, shift = _ssd_constants()
    gtril, cmask = _gla_constants()
    band = jnp.asarray(np.stack([_band_matrix(range(-(w // 2), w - w // 2)) for w in POOL_WINDOWS]), BF16)
    xprev, xnext = _halo_specs(tl, SSD_CONV_CH, n_tiles, reverse=True)
    pprev, pnext = _halo_specs(tl, BRANCH_WIDTH, n_tiles, reverse=True)
    rev = lambda width: _tile_spec(tl, width, n_tiles)
    fwd = lambda width: _tile_spec(tl, width)
    act = lambda width: jax.ShapeDtypeStruct((b, l, width), BF16)
    ssd_state = jax.ShapeDtypeStruct((b, l // SSD_CHUNK, SSD_STATE, BRANCH_WIDTH), BF16)
    gla_state = jax.ShapeDtypeStruct((b, l // GLA_CHUNK, GLA_KEY_DIM, BRANCH_WIDTH), BF16)
    xbc, pin = seq("xbc"), seq("pool")
    yp, cs, e, ssf, dec, ssb, oi, qd, gsf, tots, gsb, pool_out = pl.pallas_call(
        functools.partial(_local_kernel, seq_len=l),
        grid=(b, n_tiles),
        in_specs=[rev(SSD_CONV_CH), xprev, xnext, rev(LANE), rev(GLA_QK), rev(GLA_QK), rev(BRANCH_WIDTH), rev(LANE),
                  rev(BRANCH_WIDTH), pprev, pnext,
                  _wspec(p["ssd_conv_w"], layer), _wspec(p["ssd_conv_b"], layer), _wspec(p["ssd_a_log"], layer),
                  _wspec(p["ssd_dt_bias"], layer), _wspec(p["ssd_d"], layer), _wspec(p["gla_w2"], layer),
                  _wspec(p["gla_gate_b"], layer), _wspec(p["pool_w"], layer), _wspec(p["pool_scale"], layer),
                  _cspec(stril), _cspec(striu), _cspec(ef), _cspec(eb), _cspec(shift), _cspec(gtril), _cspec(cmask),
                  _cspec(band)],
        out_specs=[rev(BRANCH_WIDTH), rev(LANE), rev(LANE), _chunk_spec(nci, SSD_STATE, BRANCH_WIDTH, n_tiles),
                   _chunk_spec(nci, 8, BRANCH_WIDTH, n_tiles), _chunk_spec(nci, SSD_STATE, BRANCH_WIDTH, n_tiles),
                   rev(BRANCH_WIDTH), rev(2 * GLA_QK), _chunk_spec(nch, GLA_KEY_DIM, BRANCH_WIDTH, n_tiles),
                   _chunk_spec(n_blk, 8, GLA_QK, n_tiles), _chunk_spec(nch, GLA_KEY_DIM, BRANCH_WIDTH, n_tiles),
                   rev(BRANCH_WIDTH)],
        out_shape=[act(BRANCH_WIDTH), act(LANE), act(LANE), ssd_state,
                   jax.ShapeDtypeStruct((b, l // SSD_CHUNK, 8, BRANCH_WIDTH), F32), ssd_state,
                   act(BRANCH_WIDTH), act(2 * GLA_QK), gla_state,
                   jax.ShapeDtypeStruct((b, l // GLA_TILE, 8, GLA_QK), F32), gla_state, act(BRANCH_WIDTH)],
        scratch_shapes=[pltpu.VMEM((tl + 2 * HALO, SSD_CONV_CH), BF16), pltpu.VMEM((tl + 2 * HALO, BRANCH_WIDTH), BF16),
                        pltpu.VMEM((SSD_STATE, BRANCH_WIDTH), F32), pltpu.VMEM((GLA_KEY_DIM, BRANCH_WIDTH), F32)],
        compiler_params=_params(2),
        name="mix_local",
    )(xbc, xbc, xbc, seq("dt"), seq("q"), seq("k"), seq("gv"), seq("glr"), pin, pin, pin,
      p["ssd_conv_w"], p["ssd_conv_b"], p["ssd_a_log"], p["ssd_dt_bias"], p["ssd_d"], p["gla_w2"], p["gla_gate_b"],
      p["pool_w"], p["pool_scale"], stril, striu, ef, eb, shift, gtril, cmask, band)

    ssd_out, gla_out = pl.pallas_call(
        _final_kernel,
        grid=(b, n_tiles),
        in_specs=[fwd(BRANCH_WIDTH), fwd(LANE), fwd(LANE), fwd(BRANCH_WIDTH), _chunk_spec(nci, SSD_STATE, BRANCH_WIDTH),
                  _chunk_spec(nci, 8, BRANCH_WIDTH), _chunk_spec(nci, SSD_STATE, BRANCH_WIDTH),
                  fwd(BRANCH_WIDTH), fwd(2 * GLA_QK), _chunk_spec(nch, GLA_KEY_DIM, BRANCH_WIDTH),
                  _chunk_spec(n_blk, 8, GLA_QK), _chunk_spec(nch, GLA_KEY_DIM, BRANCH_WIDTH), fwd(BRANCH_WIDTH),
                  _wspec(p["ssd_norm"], layer), _wspec(p["gla_norm"], layer), _cspec(ef), _cspec(eb)],
        out_specs=[fwd(BRANCH_WIDTH), fwd(BRANCH_WIDTH)],
        out_shape=[act(BRANCH_WIDTH), act(BRANCH_WIDTH)],
        scratch_shapes=[pltpu.VMEM((SSD_STATE, BRANCH_WIDTH), F32), pltpu.VMEM((GLA_KEY_DIM, BRANCH_WIDTH), F32)],
        compiler_params=_params(2),
        name="mix_final",
    )(yp, cs, e, seq("z"), ssf, dec, ssb, oi, qd, gsf, tots, gsb, seq("r"), p["ssd_norm"], p["gla_norm"], ef, eb)
    return ssd_out, pool_out, gla_out


def _sgu_stages(u_ref, v_ref, ng_ref, w_ref, bias_ref, out_ref, slot):
    gd = BRANCH_WIDTH // SGU_GROUPS
    for r in range(0, u_ref.shape[0], SGU_CHUNK):
        vf = _rms(_gelu_tanh(v_ref[r:r + SGU_CHUNK, :].astype(F32)), ng_ref[...]).astype(BF16)
        yield
        mixed = [_dot(w_ref[g], vf[:, g * gd:(g + 1) * gd]) for g in range(SGU_GROUPS)]
        uf = _gelu_tanh(u_ref[r:r + SGU_CHUNK, :].astype(F32))
        yield
        out_ref[slot, r:r + SGU_CHUNK, :] = (uf * (jnp.concatenate(mixed, axis=1) + bias_ref[...])).astype(BF16)
        yield


def _merge_ffn_stages(x_ref, branches, gate_ref, wb_ref, wo_ref, gpost_ref, gpre_ref, w1_ref, w2_ref, gffn_ref,
                      o_ref):
    merged = None
    for n, load in branches:
        term = gate_ref[:, n * D_MODEL:(n + 1) * D_MODEL].astype(F32) * _dot(load(), wb_ref[n])
        merged = term if merged is None else merged + term
        yield
    x1 = x_ref[...] + _rms(_dot(merged.astype(BF16), wo_ref[...]), gpost_ref[...])
    h2 = _rms(x1, gpre_ref[...]).astype(BF16)
    yield
    ff = None
    for c in range(0, D_FF, D_MODEL):
        hid = jnp.square(jnp.maximum(_dot(h2, w1_ref[:, c:c + D_MODEL]), 0.0)).astype(BF16)
        yield
        part = _dot(hid, w2_ref[c:c + D_MODEL, :])
        ff = part if ff is None else ff + part
        yield
    o_ref[...] = x1 + _rms(ff, gffn_ref[...])


def _merge_ffn_kernel(x_ref, b0_ref, b1_ref, u0_ref, v0_ref, un_ref, vn_ref, b3_ref, gate_ref, sng_ref, sw_ref,
                      sbias_ref, wb_ref, wo_ref, gpost_ref, gpre_ref, w1_ref, w2_ref, gffn_ref, o_ref, sgu_ref):
    i = pl.program_id(0)

    @pl.when(i == 0)
    def _():
        _run_staged([_sgu_stages(u0_ref, v0_ref, sng_ref, sw_ref, sbias_ref, sgu_ref, 0)])

    branches = ((0, lambda: b0_ref[...]), (1, lambda: b1_ref[...]), (3, lambda: b3_ref[...]),
                (2, lambda: sgu_ref[i % 2]))
    _run_staged([
        _merge_ffn_stages(x_ref, branches, gate_ref, wb_ref, wo_ref, gpost_ref, gpre_ref, w1_ref, w2_ref, gffn_ref,
                          o_ref),
        _sgu_stages(un_ref, vn_ref, sng_ref, sw_ref, sbias_ref, sgu_ref, (i + 1) % 2)])


def _merge_ffn(x2d, ssd_out, pool_out, sgu_u, sgu_v, gla_out, gate, p, layer, tm):
    n = x2d.shape[0]
    steps = n // tm
    row = lambda w: pl.BlockSpec((tm, w), lambda i: (i, 0))
    first = pl.BlockSpec((tm, BRANCH_WIDTH), lambda i: (0, 0), pipeline_mode=pl.Buffered(1))
    ahead = pl.BlockSpec((tm, BRANCH_WIDTH), lambda i: (jnp.minimum(i + 1, steps - 1), 0))
    return pl.pallas_call(
        _merge_ffn_kernel,
        grid=(steps,),
        in_specs=[row(D_MODEL), row(BRANCH_WIDTH), row(BRANCH_WIDTH), first, first, ahead, ahead, row(BRANCH_WIDTH),
                  row(N_BRANCH * D_MODEL),
                  _wspec(p["sgu_norm"], layer), _wspec(p["sgu_w"], layer), _wspec(p["sgu_bias"], layer),
                  _wspec(p["w_branch"], layer), _wspec(p["w_out"], layer), _wspec(p["norm_mix_post"], layer),
                  _wspec(p["norm_ffn_pre"], layer), _wspec(p["w_ff1"], layer), _wspec(p["w_ff2"], layer),
                  _wspec(p["norm_ffn_post"], layer)],
        out_specs=row(D_MODEL),
        out_shape=jax.ShapeDtypeStruct((n, D_MODEL), F32),
        scratch_shapes=[pltpu.VMEM((2, tm, BRANCH_WIDTH), BF16)],
        compiler_params=_params(1),
        name="merge_ffn",
    )(x2d, ssd_out, pool_out, sgu_u, sgu_v, sgu_u, sgu_v, gla_out, gate, p["sgu_norm"], p["sgu_w"], p["sgu_bias"],
      p["w_branch"], p["w_out"], p["norm_mix_post"], p["norm_ffn_pre"], p["w_ff1"], p["w_ff2"],
      p["norm_ffn_post"])


def _pad_lanes(a, width):
    return jnp.pad(a, [(0, 0)] * (a.ndim - 1) + [(0, width - a.shape[-1])])


def _prepare_params(norm_mix_pre, w_in, ssd_conv_w, ssd_conv_b, ssd_a_log, ssd_dt_bias, ssd_d, ssd_norm, pool_w,
                    pool_scale, sgu_norm, sgu_w, sgu_b, gla_gate_w2, gla_gate_b, gla_norm, w_branch, w_out,
                    norm_mix_post, norm_ffn_pre, w_ff1, w_ff2, norm_ffn_post):
    depth = w_in.shape[0]
    pieces, off = [], 0
    for _, width, padded, _ in PROJ_GROUPS:
        pieces.append(_pad_lanes(w_in[:, :, off:off + width], padded))
        off += width
    row = lambda a: a.reshape(depth, 1, -1).astype(F32)
    w2 = jnp.zeros((depth, LANE, 2 * GLA_QK), F32)
    w2 = w2.at[:, :GLA_GATE_RANK, :GLA_QK].set(gla_gate_w2[:, 0])
    w2 = w2.at[:, GLA_GATE_RANK:2 * GLA_GATE_RANK, GLA_QK:].set(gla_gate_w2[:, 1])
    return {
        "norm_mix_pre": row(norm_mix_pre),
        "w_in": jnp.concatenate(pieces, axis=-1).astype(BF16),
        "ssd_conv_w": jnp.pad(ssd_conv_w.astype(F32), [(0, 0), (0, 8 - SSD_CONV), (0, 0)]),
        "ssd_conv_b": row(ssd_conv_b),
        "ssd_a_log": _pad_lanes(row(ssd_a_log), LANE),
        "ssd_dt_bias": _pad_lanes(row(ssd_dt_bias), LANE),
        "ssd_d": row(jnp.repeat(ssd_d, SSD_HEAD_DIM, axis=-1)),
        "ssd_norm": row(ssd_norm),
        "pool_w": pool_w.astype(BF16),
        "pool_scale": row(pool_scale),
        "sgu_norm": row(sgu_norm),
        "sgu_w": sgu_w.astype(BF16),
        "sgu_bias": jnp.repeat(jnp.swapaxes(sgu_b, 1, 2), BRANCH_WIDTH // SGU_GROUPS, axis=-1).astype(F32),
        "gla_w2": w2.astype(BF16),
        "gla_gate_b": row(gla_gate_b),
        "gla_norm": row(gla_norm),
        "w_branch": w_branch.astype(BF16),
        "w_out": w_out.astype(BF16),
        "norm_mix_post": row(norm_mix_post),
        "norm_ffn_pre": row(norm_ffn_pre),
        "w_ff1": w_ff1.astype(BF16),
        "w_ff2": w_ff2.astype(BF16),
        "norm_ffn_post": row(norm_ffn_post),
    }


def _tiles(seq_len):
    return min(512, seq_len), min(512, seq_len), min(512, seq_len)


def _trunk(x, p):
    b, l, d = x.shape
    tm_proj, tm_ffn, tl = _tiles(l)
    x2d = x.reshape(b * l, d)
    for layer in range(DEPTH):
        proj = dict(zip([g[0] for g in PROJ_GROUPS], _in_proj(x2d, p["norm_mix_pre"], p["w_in"], layer, tm_proj)))
        flat = lambda a: a.reshape(b * l, BRANCH_WIDTH)
        ssd_out, pool_out, gla_out = _mixers(proj, p, layer, b, l, tl)
        x2d = _merge_ffn(x2d, flat(ssd_out), flat(pool_out), proj["u"], proj["v"], flat(gla_out), proj["gate"],
                         p, layer, tm_ffn)
    return x2d.reshape(b, l, d)


def kernel(x_prompt, x_sample, norm_mix_pre, w_in, ssd_conv_w, ssd_conv_b, ssd_a_log, ssd_dt_bias, ssd_d, ssd_norm, pool_w, pool_scale, sgu_norm, sgu_w, sgu_b, gla_gate_w2, gla_gate_b, gla_norm, w_branch, w_out, norm_mix_post, norm_ffn_pre, w_ff1, w_ff2, norm_ffn_post):
    p = _prepare_params(norm_mix_pre, w_in, ssd_conv_w, ssd_conv_b, ssd_a_log, ssd_dt_bias, ssd_d, ssd_norm, pool_w,
                        pool_scale, sgu_norm, sgu_w, sgu_b, gla_gate_w2, gla_gate_b, gla_norm, w_branch, w_out,
                        norm_mix_post, norm_ffn_pre, w_ff1, w_ff2, norm_ffn_post)
    return (_trunk(x_prompt, p), _trunk(x_sample, p))
```

```python
import functools

import jax
import jax.numpy as jnp
import numpy as np
from jax import lax
from jax.experimental import pallas as pl
from jax.experimental.pallas import tpu as pltpu

F32 = jnp.float32
BF16 = jnp.bfloat16

D_MODEL = 1024
DEPTH = 4
BRANCH_WIDTH = 512
N_BRANCH = 4
SSD_HEADS = 8
SSD_HEAD_DIM = 64
SSD_GROUPS = 2
SSD_STATE = 64
SSD_CONV = 5
SSD_CHUNK = 128
SSD_CONV_CH = 768
POOL_WINDOWS = (2, 4, 8, 16)
POOL_GROUP_DIM = 128
POOL_CHUNK = 128
SGU_CHUNK = 128
SGU_GROUPS = 4
GLA_HEADS = 4
GLA_KEY_DIM = 64
GLA_VAL_DIM = 128
GLA_GATE_RANK = 16
GLA_GATE_NORMALIZER = 16.0
GLA_CHUNK = 64
GLA_QK = GLA_HEADS * GLA_KEY_DIM
GLA_TILE = 256
GLA_TILE_CHUNKS = GLA_TILE // GLA_CHUNK
D_FF = 4096
RMS_EPS = 1e-6

LANE = 128
HALO = 16
VMEM_LIMIT = 56 * 1024 * 1024

PROJ_GROUPS = (
    ("z", 512, 512, BF16), ("xbc", 768, 768, BF16), ("dt", 16, LANE, F32), ("pool", 512, 512, BF16),
    ("u", 512, 512, BF16), ("v", 512, 512, BF16), ("q", 256, 256, BF16), ("k", 256, 256, BF16),
    ("gv", 512, 512, BF16), ("r", 512, 512, BF16), ("glr", 32, LANE, F32), ("gate", 4096, 4096, BF16),
)
N_IN_PAD = sum(g[2] for g in PROJ_GROUPS)


LOG2_E = float(np.log2(np.e))


def _sigmoid(x):
    return 1.0 / (1.0 + jnp.exp2(x * (-LOG2_E)))


def _softplus(x):
    return jnp.maximum(x, 0.0) + jnp.log(1.0 + jnp.exp(-jnp.abs(x)))


def _gelu_tanh(x):
    c = -2.0 * np.sqrt(2.0 / np.pi) * LOG2_E
    return x / (1.0 + jnp.exp2(x * (c + (c * 0.044715) * (x * x))))


def _rms(x, g):
    return x * lax.rsqrt(jnp.mean(x * x, axis=-1, keepdims=True) + RMS_EPS) * g


def _dot(a, b):
    return jnp.dot(a, b, preferred_element_type=F32)


def _dot_tn(a, b):
    return lax.dot_general(a, b, (((0,), (0,)), ((), ())), preferred_element_type=F32)


def _dot_nt(a, b):
    return lax.dot_general(a, b, (((1,), (1,)), ((), ())), preferred_element_type=F32)


def _split_bf16(x):
    hi = x.astype(BF16)
    lo = (x - hi.astype(F32)).astype(BF16)
    return hi, lo


def _dot_split_rhs(a_bf16, x):
    hi, lo = _split_bf16(x)
    return _dot(a_bf16, hi) + _dot(a_bf16, lo)


def _dot_split_lhs(x, b_bf16):
    hi, lo = _split_bf16(x)
    return _dot(hi, b_bf16) + _dot(lo, b_bf16)


def _wspec(arr, layer):
    nd = arr.ndim - 1
    return pl.BlockSpec((None,) + tuple(arr.shape[1:]), lambda *_: (layer,) + (0,) * nd,
                        pipeline_mode=pl.Buffered(1))


def _cspec(arr):
    nd = arr.ndim
    return pl.BlockSpec(tuple(arr.shape), lambda *_: (0,) * nd, pipeline_mode=pl.Buffered(1))


def _params(n_grid):
    return pltpu.CompilerParams(dimension_semantics=("arbitrary",) * n_grid, vmem_limit_bytes=VMEM_LIMIT)


def _run_staged(generators):
    live = list(generators)
    while live:
        live = [g for g in live if next(g, StopIteration) is not StopIteration]


def _in_proj_kernel(x_ref, g_ref, w_ref, *out_refs):
    h = _rms(x_ref[...], g_ref[...]).astype(BF16)
    off = 0
    for o_ref, (name, _, width, dtype) in zip(out_refs, PROJ_GROUPS):
        for c in range(0, width, 512):
            cw = min(512, width - c)
            acc = _dot(h, w_ref[:, off + c:off + c + cw])
            if name == "gate":
                acc = _sigmoid(acc)
            o_ref[:, c:c + cw] = acc.astype(dtype)
        off += width


def _in_proj(x2d, norm_g, w_in_p, layer, tm):
    n = x2d.shape[0]
    out_shape = [jax.ShapeDtypeStruct((n, g[2]), g[3]) for g in PROJ_GROUPS]
    out_specs = [pl.BlockSpec((tm, g[2]), lambda i: (i, 0)) for g in PROJ_GROUPS]
    return pl.pallas_call(
        _in_proj_kernel,
        grid=(n // tm,),
        in_specs=[pl.BlockSpec((tm, D_MODEL), lambda i: (i, 0)), _wspec(norm_g, layer), _wspec(w_in_p, layer)],
        out_specs=out_specs,
        out_shape=out_shape,
        compiler_params=_params(1),
        name="in_proj",
    )(x2d, norm_g, w_in_p)


def _tile_spec(tl, width, n_tiles=None):
    if n_tiles is None:
        return pl.BlockSpec((None, tl, width), lambda b, i: (b, i, 0))
    return pl.BlockSpec((None, tl, width), lambda b, i: (b, n_tiles - 1 - i, 0))


def _chunk_spec(per_tile, rows, width, n_tiles=None):
    if n_tiles is None:
        return pl.BlockSpec((None, per_tile, rows, width), lambda b, i: (b, i, 0, 0))
    return pl.BlockSpec((None, per_tile, rows, width), lambda b, i: (b, n_tiles - 1 - i, 0, 0))


def _halo_specs(tl, width, n_tiles, reverse=False):
    per = tl // HALO
    last = n_tiles * per - 1

    def tile(i):
        return n_tiles - 1 - i if reverse else i

    prev = pl.BlockSpec((None, HALO, width), lambda b, i: (b, jnp.maximum(tile(i) * per - 1, 0), 0))
    nxt = pl.BlockSpec((None, HALO, width), lambda b, i: (b, jnp.minimum((tile(i) + 1) * per, last), 0))
    return prev, nxt


def _fill_halo_buffer(ext_ref, cur_ref, prev_ref, next_ref, is_first, is_last):
    tl = cur_ref.shape[0]
    zeros = jnp.zeros((HALO, ext_ref.shape[1]), ext_ref.dtype)
    ext_ref[0:HALO, :] = jnp.where(is_first, zeros, prev_ref[...])
    ext_ref[HALO:HALO + tl, :] = cur_ref[...]
    ext_ref[HALO + tl:2 * HALO + tl, :] = jnp.where(is_last, zeros, next_ref[...])


HALO_WINDOW = 128 + 2 * HALO


def _band_matrix(offsets):
    r = np.arange(128)[:, None]
    w = np.arange(HALO_WINDOW)[None, :]
    return sum((w == r + HALO + off).astype(np.float32) for off in offsets)


def _ssd_conv(xe_ref, r0, shift_ref, cw_ref, cb_ref):
    t = SSD_CHUNK
    pad = SSD_CONV // 2
    win = xe_ref[pl.ds(r0, HALO_WINDOW), :]
    shifted = _dot(shift_ref[...], win)
    acc = cb_ref[...] + cw_ref[pad:pad + 1, :] * win[HALO:HALO + t, :].astype(F32)
    for n, j in enumerate(j for j in range(SSD_CONV) if j != pad):
        acc = acc + cw_ref[j:j + 1, :] * shifted[n * t:(n + 1) * t, :]
    return acc * _sigmoid(acc)


def _ssd_decay_terms(dt_raw, dtb_ref, alog_ref, tril_ref, triu_ref):
    t = dt_raw.shape[0]
    lane = lax.broadcasted_iota(jnp.int32, (t, LANE), 1)
    used = lane < 2 * SSD_HEADS
    dt = jnp.where(used, _softplus(dt_raw + dtb_ref[...]), 0.0)
    a = jnp.where(used[0:1], -jnp.exp(alog_ref[...]), 0.0)
    dta = dt * a
    cf = _dot_split_rhs(tril_ref[...], dta)
    cr = _dot_split_rhs(triu_ref[...], dta)
    fwd = lane < SSD_HEADS
    acs = jnp.where(fwd, cf, cr)
    tot = jnp.where(fwd[0:1], cf[t - 1:t, :], cr[0:1, :])
    return dt, acs, tot


def _ssd_compact(full):
    half = BRANCH_WIDTH // SSD_GROUPS
    return jnp.concatenate([full[:SSD_STATE, :half], full[SSD_STATE:, half:]], axis=1)


def _ssd_expand(compact):
    half = BRANCH_WIDTH // SSD_GROUPS
    z = jnp.zeros((SSD_STATE, half), compact.dtype)
    return jnp.concatenate([jnp.concatenate([compact[:, :half], z], axis=1),
                            jnp.concatenate([z, compact[:, half:]], axis=1)], axis=0)


def _ssd_local_stages(xbc_ref, xprev_ref, xnext_ref, dt_ref, cw_ref, cb_ref, alog_ref, dtb_ref, dexp_ref,
                      tril_ref, triu_ref, ef_ref, eb_ref, shift_ref,
                      yp_ref, cs_ref, e_ref, sf_ref, dec_ref, sb_ref, xe_ref, s_ref):
    i = pl.program_id(1)
    n = pl.num_programs(1)
    nci = xbc_ref.shape[0] // SSD_CHUNK
    t = SSD_CHUNK

    @pl.when(i == 0)
    def _():
        s_ref[...] = jnp.zeros_like(s_ref)

    _fill_halo_buffer(xe_ref, xbc_ref, xprev_ref, xnext_ref, i == n - 1, i == 0)

    lane = lax.broadcasted_iota(jnp.int32, (t, LANE), 1)
    low = lane < SSD_STATE
    row = lax.broadcasted_iota(jnp.int32, (t, t), 0)
    col = lax.broadcasted_iota(jnp.int32, (t, t), 1)
    tge = row >= col
    tle = row <= col
    hpg = SSD_HEADS // SSD_GROUPS

    chunks = range(nci)
    xc = []
    for c in chunks:
        xc.append(_ssd_conv(xe_ref, c * t, shift_ref, cw_ref, cb_ref))
        yield
    xs = [a[:, :BRANCH_WIDTH] for a in xc]
    bs = [a[:, BRANCH_WIDTH:BRANCH_WIDTH + LANE].astype(BF16) for a in xc]
    cs = [a[:, BRANCH_WIDTH + LANE:] for a in xc]
    terms = [_ssd_decay_terms(dt_ref[pl.ds(c * t, t), :], dtb_ref, alog_ref, tril_ref, triu_ref) for c in chunks]
    dt = [a[0] for a in terms]
    acs = [a[1] for a in terms]
    tot = [a[2] for a in terms]
    yield
    cb = [(_dot_nt(jnp.where(low, cs[c], 0.0).astype(BF16), bs[c]),
           _dot_nt(jnp.where(low, 0.0, cs[c]).astype(BF16), bs[c])) for c in chunks]
    acs_t = [a.T for a in acs]
    dt_t = [a.T for a in dt]
    yield
    w = [(jnp.exp(tot[c] - acs[c]) * dt[c]).astype(BF16) for c in chunks]
    w_f = [_dot(w[c], ef_ref[...]) for c in chunks]
    w_b = [_dot(w[c], eb_ref[...]) for c in chunks]
    dec8 = [jnp.broadcast_to(jnp.exp(tot[c]), (8, LANE)) for c in chunks]
    dec_f = [_dot_split_lhs(dec8[c], ef_ref[...]) for c in chunks]
    dec_b = [_dot_split_lhs(dec8[c], eb_ref[...]) for c in chunks]
    yield

    y_pairs = [[] for _ in chunks]
    for pair in range(SSD_HEADS // 2):
        acc = [None for _ in chunks]
        for sub in range(2):
            h = 2 * pair + sub
            hb = SSD_HEADS + h
            half_mask = low if sub == 0 else jnp.logical_not(low)
            for c in chunks:
                arg = jnp.where(tge, acs[c][:, h:h + 1] - acs_t[c][h:h + 1, :],
                                acs[c][:, hb:hb + 1] - acs_t[c][hb:hb + 1, :])
                wgt = jnp.where(tge, dt_t[c][h:h + 1, :], 0.0) + jnp.where(tle, dt_t[c][hb:hb + 1, :], 0.0)
                m = (cb[c][h // hpg] * jnp.exp(arg) * wgt).astype(BF16)
                x_half = jnp.where(half_mask, xs[c][:, pair * LANE:(pair + 1) * LANE], 0.0).astype(BF16)
                part = _dot(m, x_half)
                acc[c] = part if acc[c] is None else acc[c] + part
            yield
        for c in chunks:
            y_pairs[c].append(acc[c])

    st_f = [_ssd_compact(_dot_tn(bs[c], (xs[c] * w_f[c]).astype(BF16))) for c in chunks]
    st_b = [_ssd_compact(_dot_tn(bs[c], (xs[c] * w_b[c]).astype(BF16))) for c in chunks]
    yield
    srow = lax.broadcasted_iota(jnp.int32, (8, BRANCH_WIDTH), 0)
    for c in chunks:
        rows = pl.ds(c * t, t)
        y = jnp.concatenate(y_pairs[c], axis=1) + xs[c] * dexp_ref[...]
        yp_ref[rows, :] = y.astype(yp_ref.dtype)
        cs_ref[rows, :] = cs[c].astype(cs_ref.dtype)
        e_ref[rows, :] = jnp.exp(acs[c]).astype(e_ref.dtype)
        dec_ref[c] = jnp.where(srow == 0, dec_f[c], dec_b[c])
        sf_ref[c] = st_f[c].astype(sf_ref.dtype)
        yield
    for c in reversed(chunks):
        sb_ref[c] = s_ref[...].astype(sb_ref.dtype)
        s_ref[...] = s_ref[...] * dec_b[c][0:1, :] + st_b[c]


def _ssd_final_stages(yp_ref, cs_ref, e_ref, z_ref, sf_ref, dec_ref, sb_ref, ng_ref, ef_ref, eb_ref, o_ref, s_ref):
    i = pl.program_id(1)
    nci = yp_ref.shape[0] // SSD_CHUNK
    t = SSD_CHUNK

    @pl.when(i == 0)
    def _():
        s_ref[...] = jnp.zeros_like(s_ref)

    half = BRANCH_WIDTH // SSD_GROUPS
    ng = ng_ref[...]
    chunks = range(nci)
    rows = [pl.ds(c * t, t) for c in chunks]
    s_f = []
    for c in chunks:
        s_f.append(s_ref[...].astype(BF16))
        s_ref[...] = s_ref[...] * dec_ref[c][0:1, :] + sf_ref[c].astype(F32)
    yield
    ex_f = [_dot(e_ref[rows[c], :], ef_ref[...]) for c in chunks]
    ex_b = [_dot(e_ref[rows[c], :], eb_ref[...]) for c in chunks]
    yield
    off_f = [_dot(cs_ref[rows[c], :], _ssd_expand(s_f[c])) for c in chunks]
    off_b = [_dot(cs_ref[rows[c], :], _ssd_expand(sb_ref[c])) for c in chunks]
    yield
    for c in chunks:
        y = yp_ref[rows[c], :].astype(F32) + ex_f[c] * off_f[c] + ex_b[c] * off_b[c]
        zf = z_ref[rows[c], :].astype(F32)
        y = y * (zf * _sigmoid(zf))
        y = jnp.concatenate([_rms(y[:, g * half:(g + 1) * half], ng[:, g * half:(g + 1) * half])
                             for g in range(SSD_GROUPS)], axis=1)
        o_ref[rows[c], :] = y.astype(o_ref.dtype)
        yield


def _ssd_constants():
    t = SSD_CHUNK
    r = np.arange(t)
    tril = (r[:, None] >= r[None, :]).astype(np.float32)
    triu = (r[:, None] <= r[None, :]).astype(np.float32)
    col = np.arange(BRANCH_WIDTH)
    ef = (r[:, None] == (col[None, :] // SSD_HEAD_DIM)).astype(np.float32)
    eb = (r[:, None] == (col[None, :] // SSD_HEAD_DIM) + SSD_HEADS).astype(np.float32)
    pad = SSD_CONV // 2
    shift = np.concatenate([_band_matrix([j - pad]) for j in range(SSD_CONV) if j != pad], axis=0)
    return [jnp.asarray(a, BF16) for a in (tril, triu, ef, eb, shift)]


def _pool_stages(p_ref, pprev_ref, pnext_ref, band_ref, w_ref, scale_ref, o_ref, pe_ref, seq_len):
    i = pl.program_id(1)
    n = pl.num_programs(1)
    tl = p_ref.shape[0]
    t = POOL_CHUNK
    _fill_halo_buffer(pe_ref, p_ref, pprev_ref, pnext_ref, i == n - 1, i == 0)
    tile_start = (n - 1 - i) * tl
    blocks = [(r0, g) for r0 in range(0, tl, t) for g in range(len(POOL_WINDOWS))]
    cols = lambda g: slice(g * POOL_GROUP_DIM, (g + 1) * POOL_GROUP_DIM)
    sums = []
    for r0, g in blocks:
        sums.append(_dot(band_ref[g], pe_ref[r0:r0 + HALO_WINDOW, cols(g)]))
        if g == len(POOL_WINDOWS) - 1:
            yield
    pooled = []
    for (r0, g), acc in zip(blocks, sums):
        left = POOL_WINDOWS[g] // 2
        right = POOL_WINDOWS[g] - 1 - left
        pos = tile_start + r0 + lax.broadcasted_iota(jnp.int32, (t, 1), 0)
        cnt = (jnp.minimum(pos + right, seq_len - 1) - jnp.maximum(pos - left, 0) + 1).astype(F32)
        pooled.append((acc / cnt - pe_ref[HALO + r0:HALO + r0 + t, cols(g)].astype(F32)).astype(BF16))
        if g == len(POOL_WINDOWS) - 1:
            yield
    mixed = []
    for (r0, g), pb in zip(blocks, pooled):
        mixed.append(_dot(pb, w_ref[g]))
        if g == len(POOL_WINDOWS) - 1:
            yield
    for (r0, g), y in zip(blocks, mixed):
        o_ref[r0:r0 + t, cols(g)] = (y * scale_ref[:, cols(g)]).astype(o_ref.dtype)
    yield


def _gla_decay_matrix(ecol, col):
    return jnp.concatenate(
        [jnp.broadcast_to(ecol[h * GLA_KEY_DIM:(h + 1) * GLA_KEY_DIM, col:col + 1], (GLA_KEY_DIM, GLA_VAL_DIM))
         for h in range(GLA_HEADS)], axis=1)


def _gla_block_diag(compact):
    z = jnp.zeros((GLA_KEY_DIM, GLA_VAL_DIM), compact.dtype)
    return jnp.concatenate(
        [jnp.concatenate([compact[:, h * GLA_VAL_DIM:(h + 1) * GLA_VAL_DIM] if g == h else z
                          for g in range(GLA_HEADS)], axis=1) for h in range(GLA_HEADS)], axis=0)


def _gla_ecol(tots):
    padded = jnp.concatenate([tots, jnp.zeros((LANE - tots.shape[0], GLA_QK), F32)], axis=0)
    return jnp.exp(padded.T)


def _gla_local_block(blk, q_ref, k_ref, v_ref, glr_ref, w2_ref, gb_ref, tril_ref, cmask_ref, oi_ref, qd_ref,
                     tots_ref, results):
    t = GLA_CHUNK
    nch = GLA_TILE_CHUNKS
    tile = GLA_TILE
    rows = pl.ds(blk * tile, tile)

    pre = _dot(glr_ref[rows, :].astype(BF16), w2_ref[...]) + gb_ref[...]
    yield
    gk = -_softplus(-pre) * (1.0 / GLA_GATE_NORMALIZER)
    bc = _dot(tril_ref[...], gk.astype(BF16))
    yield
    last = [bc[c * t + t - 1:c * t + t, :] for c in range(nch)]
    tot = jnp.concatenate([jnp.broadcast_to(row, (t, 2 * GLA_QK)) for row in last], axis=0)
    bc_f = bc[:, :GLA_QK]
    bc_b = bc[:, GLA_QK:]
    rb = tot[:, GLA_QK:] - bc_b + gk[:, GLA_QK:]
    qf = q_ref[rows, :].astype(F32) * (GLA_KEY_DIM ** -0.5)
    kf = k_ref[rows, :].astype(F32)
    v = v_ref[rows, :]
    qd_f = qf * jnp.exp(bc_f)
    qd_b = qf * jnp.exp(rb)
    ki_f = (kf * jnp.exp(-bc_f)).astype(BF16)
    ki_b = (kf * jnp.exp(-rb)).astype(BF16)
    ke_f = kf * jnp.exp(tot[:, :GLA_QK] - bc_f)
    ke_b = kf * jnp.exp(bc_b - gk[:, GLA_QK:])
    qd_ref[rows, :GLA_QK] = qd_f.astype(qd_ref.dtype)
    qd_ref[rows, GLA_QK:] = qd_b.astype(qd_ref.dtype)

    lane = lax.broadcasted_iota(jnp.int32, (tile, GLA_QK), 1)

    def stack_heads(qd):
        return jnp.concatenate(
            [jnp.where((lane >= h * GLA_KEY_DIM) & (lane < (h + 1) * GLA_KEY_DIM), qd, 0.0)
             for h in range(GLA_HEADS)], axis=0).astype(BF16)

    att_f = _dot_nt(stack_heads(qd_f), ki_f)
    att_b = _dot_nt(stack_heads(qd_b), ki_b)
    yield
    row = lax.broadcasted_iota(jnp.int32, (tile, tile), 0)
    col = lax.broadcasted_iota(jnp.int32, (tile, tile), 1)
    same = (row & -t) == (col & -t)
    m_f = same & (row >= col)
    m_b = same & (row <= col)
    for h in range(GLA_HEADS):
        hrows = slice(h * tile, (h + 1) * tile)
        att = (jnp.where(m_f, att_f[hrows, :], 0.0) + jnp.where(m_b, att_b[hrows, :], 0.0)).astype(BF16)
        oi_ref[rows, h * GLA_VAL_DIM:(h + 1) * GLA_VAL_DIM] = _dot(
            att, v[:, h * GLA_VAL_DIM:(h + 1) * GLA_VAL_DIM]).astype(oi_ref.dtype)
    yield
    ket_f = ke_f.T.astype(BF16)
    ket_b = ke_b.T.astype(BF16)
    cmask = cmask_ref[...]
    st = []
    for h in range(GLA_HEADS):
        vh = v[:, h * GLA_VAL_DIM:(h + 1) * GLA_VAL_DIM]
        v_exp = jnp.concatenate([vh] * nch, axis=1) * cmask
        lhs = jnp.concatenate([ket_f[h * GLA_KEY_DIM:(h + 1) * GLA_KEY_DIM, :],
                               ket_b[h * GLA_KEY_DIM:(h + 1) * GLA_KEY_DIM, :]], axis=0)
        st.append(_dot(lhs, v_exp))
    tots = jnp.concatenate([r[:, :GLA_QK] for r in last] + [r[:, GLA_QK:] for r in last], axis=0)
    tots_ref[blk] = tots
    results.append((blk, _gla_ecol(tots), st))
    yield


def _gla_local_stages(q_ref, k_ref, v_ref, glr_ref, w2_ref, gb_ref, tril_ref, cmask_ref,
                      oi_ref, qd_ref, sf_ref, tots_ref, sb_ref, s_ref):
    i = pl.program_id(1)
    nch = GLA_TILE_CHUNKS
    n_blk = q_ref.shape[0] // GLA_TILE

    @pl.when(i == 0)
    def _():
        s_ref[...] = jnp.zeros_like(s_ref)

    results = []
    live = [_gla_local_block(blk, q_ref, k_ref, v_ref, glr_ref, w2_ref, gb_ref, tril_ref, cmask_ref, oi_ref,
                             qd_ref, tots_ref, results) for blk in range(n_blk)]
    while live:
        live = [g for g in live if next(g, StopIteration) is not StopIteration]
        yield

    for blk, ecol, st in sorted(results, key=lambda r: -r[0]):
        def local_state(c, lo):
            return jnp.concatenate([st[h][lo:lo + GLA_KEY_DIM, c * GLA_VAL_DIM:(c + 1) * GLA_VAL_DIM]
                                    for h in range(GLA_HEADS)], axis=1)

        for c in reversed(range(nch)):
            sf_ref[blk * nch + c] = local_state(c, 0).astype(sf_ref.dtype)
            sb_ref[blk * nch + c] = s_ref[...].astype(sb_ref.dtype)
            s_ref[...] = s_ref[...] * _gla_decay_matrix(ecol, nch + c) + local_state(c, GLA_KEY_DIM)
        yield


def _gla_final_stages(oi_ref, qd_ref, sf_ref, tots_ref, sb_ref, r_ref, ng_ref, o_ref, s_ref):
    i = pl.program_id(1)
    t = GLA_CHUNK
    nch = GLA_TILE_CHUNKS
    n_blk = oi_ref.shape[0] // GLA_TILE

    @pl.when(i == 0)
    def _():
        s_ref[...] = jnp.zeros_like(s_ref)

    ng = ng_ref[...]
    ecol = [_gla_ecol(tots_ref[blk]) for blk in range(n_blk)]
    chunks = range(n_blk * nch)
    rows = [pl.ds(c * t, t) for c in chunks]
    s_f = []
    for c in chunks:
        s_f.append(s_ref[...].astype(BF16))
        s_ref[...] = s_ref[...] * _gla_decay_matrix(ecol[c // nch], c % nch) + sf_ref[c].astype(F32)
    yield
    inter_f = [_dot(qd_ref[rows[c], :GLA_QK], _gla_block_diag(s_f[c])) for c in chunks]
    yield
    inter_b = [_dot(qd_ref[rows[c], GLA_QK:], _gla_block_diag(sb_ref[c])) for c in chunks]
    yield
    for blk in range(n_blk):
        brows = pl.ds(blk * GLA_TILE, GLA_TILE)
        o = jnp.concatenate([oi_ref[rows[c], :].astype(F32) + inter_f[c] + inter_b[c]
                             for c in range(blk * nch, (blk + 1) * nch)], axis=0)
        rf = r_ref[brows, :].astype(F32)
        o = jnp.concatenate([_rms(o[:, h * GLA_VAL_DIM:(h + 1) * GLA_VAL_DIM], ng[:, h * GLA_VAL_DIM:(h + 1) * GLA_VAL_DIM])
                             for h in range(GLA_HEADS)], axis=1)
        o_ref[brows, :] = (o * (rf * _sigmoid(rf))).astype(o_ref.dtype)
        yield


def _gla_constants():
    r = np.arange(GLA_TILE)
    tril = ((r[:, None] >= r[None, :]) & (r[:, None] // GLA_CHUNK == r[None, :] // GLA_CHUNK)).astype(np.float32)
    ccol = np.arange(GLA_TILE_CHUNKS * GLA_VAL_DIM) // GLA_VAL_DIM
    cmask = (r[:, None] // GLA_CHUNK == ccol[None, :]).astype(np.float32)
    return jnp.asarray(tril, BF16), jnp.asarray(cmask, BF16)


def _local_kernel(xbc_ref, xprev_ref, xnext_ref, dt_ref, q_ref, k_ref, v_ref, glr_ref, pin_ref, pprev_ref, pnext_ref,
                  cw_ref, cb_ref, alog_ref, dtb_ref, dexp_ref, w2_ref, gb_ref, pw_ref, pscale_ref,
                  stril_ref, striu_ref, ef_ref, eb_ref, shift_ref, gtril_ref, cmask_ref, band_ref,
                  yp_ref, cs_ref, e_ref, ssf_ref, dec_ref, ssb_ref, oi_ref, qd_ref, gsf_ref, tots_ref, gsb_ref,
                  pout_ref, xe_ref, pe_ref, s_ssd_ref, s_gla_ref, *, seq_len):
    _run_staged([
        _ssd_local_stages(xbc_ref, xprev_ref, xnext_ref, dt_ref, cw_ref, cb_ref, alog_ref, dtb_ref, dexp_ref,
                          stril_ref, striu_ref, ef_ref, eb_ref, shift_ref,
                          yp_ref, cs_ref, e_ref, ssf_ref, dec_ref, ssb_ref, xe_ref, s_ssd_ref),
        _gla_local_stages(q_ref, k_ref, v_ref, glr_ref, w2_ref, gb_ref, gtril_ref, cmask_ref,
                          oi_ref, qd_ref, gsf_ref, tots_ref, gsb_ref, s_gla_ref),
        _pool_stages(pin_ref, pprev_ref, pnext_ref, band_ref, pw_ref, pscale_ref, pout_ref, pe_ref, seq_len),
    ])


def _final_kernel(yp_ref, cs_ref, e_ref, z_ref, ssf_ref, dec_ref, ssb_ref, oi_ref, qd_ref, gsf_ref, tots_ref, gsb_ref,
                  r_ref, sng_ref, gng_ref, ef_ref, eb_ref, ssd_out_ref, gla_out_ref, s_ssd_ref, s_gla_ref):
    _run_staged([
        _ssd_final_stages(yp_ref, cs_ref, e_ref, z_ref, ssf_ref, dec_ref, ssb_ref, sng_ref, ef_ref, eb_ref,
                          ssd_out_ref, s_ssd_ref),
        _gla_final_stages(oi_ref, qd_ref, gsf_ref, tots_ref, gsb_ref, r_ref, gng_ref, gla_out_ref, s_gla_ref),
    ])


def _mixers(proj, p, layer, b, l, tl):
    seq = lambda name: proj[name].reshape(b, l, -1)
    n_tiles = l // tl
    nci = tl // SSD_CHUNK
    n_blk = tl // GLA_TILE
    nch = n_blk * GLA_TILE_CHUNKS
    stril, striu, ef, eb, shift = _ssd_constants()
    gtril, cmask = _gla_constants()
    band = jnp.asarray(np.stack([_band_matrix(range(-(w // 2), w - w // 2)) for w in POOL_WINDOWS]), BF16)
    xprev, xnext = _halo_specs(tl, SSD_CONV_CH, n_tiles, reverse=True)
    pprev, pnext = _halo_specs(tl, BRANCH_WIDTH, n_tiles, reverse=True)
    rev = lambda width: _tile_spec(tl, width, n_tiles)
    fwd = lambda width: _tile_spec(tl, width)
    act = lambda width: jax.ShapeDtypeStruct((b, l, width), BF16)
    ssd_state = jax.ShapeDtypeStruct((b, l // SSD_CHUNK, SSD_STATE, BRANCH_WIDTH), BF16)
    gla_state = jax.ShapeDtypeStruct((b, l // GLA_CHUNK, GLA_KEY_DIM, BRANCH_WIDTH), BF16)
    xbc, pin = seq("xbc"), seq("pool")
    yp, cs, e, ssf, dec, ssb, oi, qd, gsf, tots, gsb, pool_out = pl.pallas_call(
        functools.partial(_local_kernel, seq_len=l),
        grid=(b, n_tiles),
        in_specs=[rev(SSD_CONV_CH), xprev, xnext, rev(LANE), rev(GLA_QK), rev(GLA_QK), rev(BRANCH_WIDTH), rev(LANE),
                  rev(BRANCH_WIDTH), pprev, pnext,
                  _wspec(p["ssd_conv_w"], layer), _wspec(p["ssd_conv_b"], layer), _wspec(p["ssd_a_log"], layer),
                  _wspec(p["ssd_dt_bias"], layer), _wspec(p["ssd_d"], layer), _wspec(p["gla_w2"], layer),
                  _wspec(p["gla_gate_b"], layer), _wspec(p["pool_w"], layer), _wspec(p["pool_scale"], layer),
                  _cspec(stril), _cspec(striu), _cspec(ef), _cspec(eb), _cspec(shift), _cspec(gtril), _cspec(cmask),
                  _cspec(band)],
        out_specs=[rev(BRANCH_WIDTH), rev(LANE), rev(LANE), _chunk_spec(nci, SSD_STATE, BRANCH_WIDTH, n_tiles),
                   _chunk_spec(nci, 8, BRANCH_WIDTH, n_tiles), _chunk_spec(nci, SSD_STATE, BRANCH_WIDTH, n_tiles),
                   rev(BRANCH_WIDTH), rev(2 * GLA_QK), _chunk_spec(nch, GLA_KEY_DIM, BRANCH_WIDTH, n_tiles),
                   _chunk_spec(n_blk, 8, GLA_QK, n_tiles), _chunk_spec(nch, GLA_KEY_DIM, BRANCH_WIDTH, n_tiles),
                   rev(BRANCH_WIDTH)],
        out_shape=[act(BRANCH_WIDTH), act(LANE), act(LANE), ssd_state,
                   jax.ShapeDtypeStruct((b, l // SSD_CHUNK, 8, BRANCH_WIDTH), F32), ssd_state,
                   act(BRANCH_WIDTH), act(2 * GLA_QK), gla_state,
                   jax.ShapeDtypeStruct((b, l // GLA_TILE, 8, GLA_QK), F32), gla_state, act(BRANCH_WIDTH)],
        scratch_shapes=[pltpu.VMEM((tl + 2 * HALO, SSD_CONV_CH), BF16), pltpu.VMEM((tl + 2 * HALO, BRANCH_WIDTH), BF16),
                        pltpu.VMEM((SSD_STATE, BRANCH_WIDTH), F32), pltpu.VMEM((GLA_KEY_DIM, BRANCH_WIDTH), F32)],
        compiler_params=_params(2),
        name="mix_local",
    )(xbc, xbc, xbc, seq("dt"), seq("q"), seq("k"), seq("gv"), seq("glr"), pin, pin, pin,
      p["ssd_conv_w"], p["ssd_conv_b"], p["ssd_a_log"], p["ssd_dt_bias"], p["ssd_d"], p["gla_w2"], p["gla_gate_b"],
      p["pool_w"], p["pool_scale"], stril, striu, ef, eb, shift, gtril, cmask, band)

    ssd_out, gla_out = pl.pallas_call(
        _final_kernel,
        grid=(b, n_tiles),
        in_specs=[fwd(BRANCH_WIDTH), fwd(LANE), fwd(LANE), fwd(BRANCH_WIDTH), _chunk_spec(nci, SSD_STATE, BRANCH_WIDTH),
                  _chunk_spec(nci, 8, BRANCH_WIDTH), _chunk_spec(nci, SSD_STATE, BRANCH_WIDTH),
                  fwd(BRANCH_WIDTH), fwd(2 * GLA_QK), _chunk_spec(nch, GLA_KEY_DIM, BRANCH_WIDTH),
                  _chunk_spec(n_blk, 8, GLA_QK), _chunk_spec(nch, GLA_KEY_DIM, BRANCH_WIDTH), fwd(BRANCH_WIDTH),
                  _wspec(p["ssd_norm"], layer), _wspec(p["gla_norm"], layer), _cspec(ef), _cspec(eb)],
        out_specs=[fwd(BRANCH_WIDTH), fwd(BRANCH_WIDTH)],
        out_shape=[act(BRANCH_WIDTH), act(BRANCH_WIDTH)],
        scratch_shapes=[pltpu.VMEM((SSD_STATE, BRANCH_WIDTH), F32), pltpu.VMEM((GLA_KEY_DIM, BRANCH_WIDTH), F32)],
        compiler_params=_params(2),
        name="mix_final",
    )(yp, cs, e, seq("z"), ssf, dec, ssb, oi, qd, gsf, tots, gsb, seq("r"), p["ssd_norm"], p["gla_norm"], ef, eb)
    return ssd_out, pool_out, gla_out


def _sgu_stages(u_ref, v_ref, ng_ref, w_ref, bias_ref, out_ref, slot):
    gd = BRANCH_WIDTH // SGU_GROUPS
    for r in range(0, u_ref.shape[0], SGU_CHUNK):
        vf = _rms(_gelu_tanh(v_ref[r:r + SGU_CHUNK, :].astype(F32)), ng_ref[...]).astype(BF16)
        yield
        mixed = [_dot(w_ref[g], vf[:, g * gd:(g + 1) * gd]) for g in range(SGU_GROUPS)]
        uf = _gelu_tanh(u_ref[r:r + SGU_CHUNK, :].astype(F32))
        yield
        out_ref[slot, r:r + SGU_CHUNK, :] = (uf * (jnp.concatenate(mixed, axis=1) + bias_ref[...])).astype(BF16)
        yield


def _merge_ffn_stages(x_ref, branches, gate_ref, wb_ref, wo_ref, gpost_ref, gpre_ref, w1_ref, w2_ref, gffn_ref,
                      o_ref):
    merged = None
    for n, load in branches:
        term = gate_ref[:, n * D_MODEL:(n + 1) * D_MODEL].astype(F32) * _dot(load(), wb_ref[n])
        merged = term if merged is None else merged + term
        yield
    x1 = x_ref[...] + _rms(_dot(merged.astype(BF16), wo_ref[...]), gpost_ref[...])
    h2 = _rms(x1, gpre_ref[...]).astype(BF16)
    yield
    ff = None
    for c in range(0, D_FF, D_MODEL):
        hid = jnp.square(jnp.maximum(_dot(h2, w1_ref[:, c:c + D_MODEL]), 0.0)).astype(BF16)
        yield
        part = _dot(hid, w2_ref[c:c + D_MODEL, :])
        ff = part if ff is None else ff + part
        yield
    o_ref[...] = x1 + _rms(ff, gffn_ref[...])


def _merge_ffn_kernel(x_ref, b0_ref, b1_ref, u0_ref, v0_ref, un_ref, vn_ref, b3_ref, gate_ref, sng_ref, sw_ref,
                      sbias_ref, wb_ref, wo_ref, gpost_ref, gpre_ref, w1_ref, w2_ref, gffn_ref, o_ref, sgu_ref):
    i = pl.program_id(0)

    @pl.when(i == 0)
    def _():
        _run_staged([_sgu_stages(u0_ref, v0_ref, sng_ref, sw_ref, sbias_ref, sgu_ref, 0)])

    branches = ((0, lambda: b0_ref[...]), (1, lambda: b1_ref[...]), (3, lambda: b3_ref[...]),
                (2, lambda: sgu_ref[i % 2]))
    _run_staged([
        _merge_ffn_stages(x_ref, branches, gate_ref, wb_ref, wo_ref, gpost_ref, gpre_ref, w1_ref, w2_ref, gffn_ref,
                          o_ref),
        _sgu_stages(un_ref, vn_ref, sng_ref, sw_ref, sbias_ref, sgu_ref, (i + 1) % 2)])


def _merge_ffn(x2d, ssd_out, pool_out, sgu_u, sgu_v, gla_out, gate, p, layer, tm):
    n = x2d.shape[0]
    steps = n // tm
    row = lambda w: pl.BlockSpec((tm, w), lambda i: (i, 0))
    first = pl.BlockSpec((tm, BRANCH_WIDTH), lambda i: (0, 0), pipeline_mode=pl.Buffered(1))
    ahead = pl.BlockSpec((tm, BRANCH_WIDTH), lambda i: (jnp.minimum(i + 1, steps - 1), 0))
    return pl.pallas_call(
        _merge_ffn_kernel,
        grid=(steps,),
        in_specs=[row(D_MODEL), row(BRANCH_WIDTH), row(BRANCH_WIDTH), first, first, ahead, ahead, row(BRANCH_WIDTH),
                  row(N_BRANCH * D_MODEL),
                  _wspec(p["sgu_norm"], layer), _wspec(p["sgu_w"], layer), _wspec(p["sgu_bias"], layer),
                  _wspec(p["w_branch"], layer), _wspec(p["w_out"], layer), _wspec(p["norm_mix_post"], layer),
                  _wspec(p["norm_ffn_pre"], layer), _wspec(p["w_ff1"], layer), _wspec(p["w_ff2"], layer),
                  _wspec(p["norm_ffn_post"], layer)],
        out_specs=row(D_MODEL),
        out_shape=jax.ShapeDtypeStruct((n, D_MODEL), F32),
        scratch_shapes=[pltpu.VMEM((2, tm, BRANCH_WIDTH), BF16)],
        compiler_params=_params(1),
        name="merge_ffn",
    )(x2d, ssd_out, pool_out, sgu_u, sgu_v, sgu_u, sgu_v, gla_out, gate, p["sgu_norm"], p["sgu_w"], p["sgu_bias"],
      p["w_branch"], p["w_out"], p["norm_mix_post"], p["norm_ffn_pre"], p["w_ff1"], p["w_ff2"],
      p["norm_ffn_post"])


def _pad_lanes(a, width):
    return jnp.pad(a, [(0, 0)] * (a.ndim - 1) + [(0, width - a.shape[-1])])


def _prepare_params(norm_mix_pre, w_in, ssd_conv_w, ssd_conv_b, ssd_a_log, ssd_dt_bias, ssd_d, ssd_norm, pool_w,
                    pool_scale, sgu_norm, sgu_w, sgu_b, gla_gate_w2, gla_gate_b, gla_norm, w_branch, w_out,
                    norm_mix_post, norm_ffn_pre, w_ff1, w_ff2, norm_ffn_post):
    depth = w_in.shape[0]
    pieces, off = [], 0
    for _, width, padded, _ in PROJ_GROUPS:
        pieces.append(_pad_lanes(w_in[:, :, off:off + width], padded))
        off += width
    row = lambda a: a.reshape(depth, 1, -1).astype(F32)
    w2 = jnp.zeros((depth, LANE, 2 * GLA_QK), F32)
    w2 = w2.at[:, :GLA_GATE_RANK, :GLA_QK].set(gla_gate_w2[:, 0])
    w2 = w2.at[:, GLA_GATE_RANK:2 * GLA_GATE_RANK, GLA_QK:].set(gla_gate_w2[:, 1])
    return {
        "norm_mix_pre": row(norm_mix_pre),
        "w_in": jnp.concatenate(pieces, axis=-1).astype(BF16),
        "ssd_conv_w": jnp.pad(ssd_conv_w.astype(F32), [(0, 0), (0, 8 - SSD_CONV), (0, 0)]),
        "ssd_conv_b": row(ssd_conv_b),
        "ssd_a_log": _pad_lanes(row(ssd_a_log), LANE),
        "ssd_dt_bias": _pad_lanes(row(ssd_dt_bias), LANE),
        "ssd_d": row(jnp.repeat(ssd_d, SSD_HEAD_DIM, axis=-1)),
        "ssd_norm": row(ssd_norm),
        "pool_w": pool_w.astype(BF16),
        "pool_scale": row(pool_scale),
        "sgu_norm": row(sgu_norm),
        "sgu_w": sgu_w.astype(BF16),
        "sgu_bias": jnp.repeat(jnp.swapaxes(sgu_b, 1, 2), BRANCH_WIDTH // SGU_GROUPS, axis=-1).astype(F32),
        "gla_w2": w2.astype(BF16),
        "gla_gate_b": row(gla_gate_b),
        "gla_norm": row(gla_norm),
        "w_branch": w_branch.astype(BF16),
        "w_out": w_out.astype(BF16),
        "norm_mix_post": row(norm_mix_post),
        "norm_ffn_pre": row(norm_ffn_pre),
        "w_ff1": w_ff1.astype(BF16),
        "w_ff2": w_ff2.astype(BF16),
        "norm_ffn_post": row(norm_ffn_post),
    }


def _tiles(seq_len):
    return min(512, seq_len), min(512, seq_len), min(1024, seq_len)


def _trunk(x, p):
    b, l, d = x.shape
    tm_proj, tm_ffn, tl = _tiles(l)
    x2d = x.reshape(b * l, d)
    for layer in range(DEPTH):
        proj = dict(zip([g[0] for g in PROJ_GROUPS], _in_proj(x2d, p["norm_mix_pre"], p["w_in"], layer, tm_proj)))
        flat = lambda a: a.reshape(b * l, BRANCH_WIDTH)
        ssd_out, pool_out, gla_out = _mixers(proj, p, layer, b, l, tl)
        x2d = _merge_ffn(x2d, flat(ssd_out), flat(pool_out), proj["u"], proj["v"], flat(gla_out), proj["gate"],
                         p, layer, tm_ffn)
    return x2d.reshape(b, l, d)


def kernel(x_prompt, x_sample, norm_mix_pre, w_in, ssd_conv_w, ssd_conv_b, ssd_a_log, ssd_dt_bias, ssd_d, ssd_norm, pool_w, pool_scale, sgu_norm, sgu_w, sgu_b, gla_gate_w2, gla_gate_b, gla_norm, w_branch, w_out, norm_mix_post, norm_ffn_pre, w_ff1, w_ff2, norm_ffn_post):
    p = _prepare_params(norm_mix_pre, w_in, ssd_conv_w, ssd_conv_b, ssd_a_log, ssd_dt_bias, ssd_d, ssd_norm, pool_w,
                        pool_scale, sgu_norm, sgu_w, sgu_b, gla_gate_w2, gla_gate_b, gla_norm, w_branch, w_out,
                        norm_mix_post, norm_ffn_pre, w_ff1, w_ff2, norm_ffn_post)
    return (_trunk(x_prompt, p), _trunk(x_sample, p))
```

```python
import functools

import jax
import jax.numpy as jnp
import numpy as np
from jax import lax
from jax.experimental import pallas as pl
from jax.experimental.pallas import tpu as pltpu

F32 = jnp.float32
BF16 = jnp.bfloat16

D_MODEL = 1024
DEPTH = 4
BRANCH_WIDTH = 512
N_BRANCH = 4
SSD_HEADS = 8
SSD_HEAD_DIM = 64
SSD_GROUPS = 2
SSD_STATE = 64
SSD_CONV = 5
SSD_CHUNK = 128
SSD_CONV_CH = 768
POOL_WINDOWS = (2, 4, 8, 16)
POOL_GROUP_DIM = 128
POOL_CHUNK = 128
SGU_CHUNK = 128
SGU_GROUPS = 4
GLA_HEADS = 4
GLA_KEY_DIM = 64
GLA_VAL_DIM = 128
GLA_GATE_RANK = 16
GLA_GATE_NORMALIZER = 16.0
GLA_CHUNK = 64
GLA_QK = GLA_HEADS * GLA_KEY_DIM
GLA_TILE = 256
GLA_TILE_CHUNKS = GLA_TILE // GLA_CHUNK
D_FF = 4096
RMS_EPS = 1e-6

LANE = 128
HALO = 16
VMEM_LIMIT = 56 * 1024 * 1024

_REF_COLS = {}
_off = 0
for _name, _width in (("z", 512), ("xbc", 768), ("dt", 16), ("pool", 512), ("u", 512), ("v", 512), ("q", 256),
                      ("k", 256), ("gv", 512), ("r", 512), ("glr", 32), ("gate", 4096)):
    _REF_COLS[_name] = (_off, _width)
    _off += _width
SSD_DT_LANES = 2 * SSD_HEADS
PROJ_GROUPS = (
    ("gate", ("gate",), 4096, BF16), ("z", ("z",), 512, BF16), ("xbc", ("xbc",), 768, BF16),
    ("dtg", ("dt", "glr"), LANE, F32), ("pool", ("pool",), 512, BF16), ("u", ("u",), 512, BF16),
    ("v", ("v",), 512, BF16), ("q", ("q",), 256, BF16), ("k", ("k",), 256, BF16), ("gv", ("gv",), 512, BF16),
    ("r", ("r",), 512, BF16),
)
PROJ_CHUNK = 512


LOG2_E = float(np.log2(np.e))


def _sigmoid(x):
    return 1.0 / (1.0 + jnp.exp2(x * (-LOG2_E)))


def _softplus(x):
    return jnp.maximum(x, 0.0) + jnp.log(1.0 + jnp.exp(-jnp.abs(x)))


def _gelu_tanh(x):
    c = -2.0 * np.sqrt(2.0 / np.pi) * LOG2_E
    return x / (1.0 + jnp.exp2(x * (c + (c * 0.044715) * (x * x))))


def _rms(x, g):
    return x * lax.rsqrt(jnp.mean(x * x, axis=-1, keepdims=True) + RMS_EPS) * g


def _dot(a, b):
    return jnp.dot(a, b, preferred_element_type=F32)


def _dot_tn(a, b):
    return lax.dot_general(a, b, (((0,), (0,)), ((), ())), preferred_element_type=F32)


def _dot_nt(a, b):
    return lax.dot_general(a, b, (((1,), (1,)), ((), ())), preferred_element_type=F32)


def _split_bf16(x):
    hi = x.astype(BF16)
    lo = (x - hi.astype(F32)).astype(BF16)
    return hi, lo


def _dot_split_rhs(a_bf16, x):
    hi, lo = _split_bf16(x)
    return _dot(a_bf16, hi) + _dot(a_bf16, lo)


def _dot_split_lhs(x, b_bf16):
    hi, lo = _split_bf16(x)
    return _dot(hi, b_bf16) + _dot(lo, b_bf16)


def _wspec(arr, layer):
    nd = arr.ndim - 1
    return pl.BlockSpec((None,) + tuple(arr.shape[1:]), lambda *_: (layer,) + (0,) * nd,
                        pipeline_mode=pl.Buffered(1))


def _cspec(arr):
    nd = arr.ndim
    return pl.BlockSpec(tuple(arr.shape), lambda *_: (0,) * nd, pipeline_mode=pl.Buffered(1))


def _params(n_grid):
    return pltpu.CompilerParams(dimension_semantics=("arbitrary",) * n_grid, vmem_limit_bytes=VMEM_LIMIT)


def _run_staged(generators):
    live = list(generators)
    while live:
        live = [g for g in live if next(g, StopIteration) is not StopIteration]


def _in_proj_kernel(x_ref, g_ref, w_ref, *out_refs):
    h = _rms(x_ref[...], g_ref[...]).astype(BF16)
    off = 0
    for o_ref, (name, _, width, dtype) in zip(out_refs, PROJ_GROUPS):
        for c in range(0, width, PROJ_CHUNK):
            cw = min(PROJ_CHUNK, width - c)
            acc = _dot(h, w_ref[:, off + c:off + c + cw])
            if name == "gate":
                acc = _sigmoid(acc)
            o_ref[:, c:c + cw] = acc.astype(dtype)
        off += width


def _in_proj(x2d, norm_g, w_in_p, layer, tm):
    n = x2d.shape[0]
    out_shape = [jax.ShapeDtypeStruct((n, g[2]), g[3]) for g in PROJ_GROUPS]
    out_specs = [pl.BlockSpec((tm, g[2]), lambda i: (i, 0)) for g in PROJ_GROUPS]
    return pl.pallas_call(
        _in_proj_kernel,
        grid=(n // tm,),
        in_specs=[pl.BlockSpec((tm, D_MODEL), lambda i: (i, 0)), _wspec(norm_g, layer), _wspec(w_in_p, layer)],
        out_specs=out_specs,
        out_shape=out_shape,
        compiler_params=_params(1),
        name="in_proj",
    )(x2d, norm_g, w_in_p)


def _tile_spec(tl, width, n_tiles=None):
    if n_tiles is None:
        return pl.BlockSpec((None, tl, width), lambda b, i: (b, i, 0))
    return pl.BlockSpec((None, tl, width), lambda b, i: (b, n_tiles - 1 - i, 0))


def _chunk_spec(per_tile, rows, width, n_tiles=None):
    if n_tiles is None:
        return pl.BlockSpec((None, per_tile, rows, width), lambda b, i: (b, i, 0, 0))
    return pl.BlockSpec((None, per_tile, rows, width), lambda b, i: (b, n_tiles - 1 - i, 0, 0))


def _halo_specs(tl, width, n_tiles, reverse=False):
    per = tl // HALO
    last = n_tiles * per - 1

    def tile(i):
        return n_tiles - 1 - i if reverse else i

    prev = pl.BlockSpec((None, HALO, width), lambda b, i: (b, jnp.maximum(tile(i) * per - 1, 0), 0))
    nxt = pl.BlockSpec((None, HALO, width), lambda b, i: (b, jnp.minimum((tile(i) + 1) * per, last), 0))
    return prev, nxt


def _fill_halo_buffer(ext_ref, cur_ref, prev_ref, next_ref, is_first, is_last):
    tl = cur_ref.shape[0]
    zeros = jnp.zeros((HALO, ext_ref.shape[1]), ext_ref.dtype)
    ext_ref[0:HALO, :] = jnp.where(is_first, zeros, prev_ref[...])
    ext_ref[HALO:HALO + tl, :] = cur_ref[...]
    ext_ref[HALO + tl:2 * HALO + tl, :] = jnp.where(is_last, zeros, next_ref[...])


HALO_WINDOW = 128 + 2 * HALO


def _band_matrix(offsets):
    r = np.arange(128)[:, None]
    w = np.arange(HALO_WINDOW)[None, :]
    return sum((w == r + HALO + off).astype(np.float32) for off in offsets)


def _ssd_conv(xe_ref, r0, shift_ref, cw_ref, cb_ref):
    t = SSD_CHUNK
    pad = SSD_CONV // 2
    win = xe_ref[pl.ds(r0, HALO_WINDOW), :]
    shifted = _dot(shift_ref[...], win)
    acc = cb_ref[...] + cw_ref[pad:pad + 1, :] * win[HALO:HALO + t, :].astype(F32)
    for n, j in enumerate(j for j in range(SSD_CONV) if j != pad):
        acc = acc + cw_ref[j:j + 1, :] * shifted[n * t:(n + 1) * t, :]
    return acc * _sigmoid(acc)


def _ssd_decay_terms(dt_raw, dtb_ref, alog_ref, tril_ref, triu_ref):
    t = dt_raw.shape[0]
    lane = lax.broadcasted_iota(jnp.int32, (t, LANE), 1)
    used = lane < 2 * SSD_HEADS
    dt = jnp.where(used, _softplus(dt_raw + dtb_ref[...]), 0.0)
    a = jnp.where(used[0:1], -jnp.exp(alog_ref[...]), 0.0)
    dta = dt * a
    cf = _dot_split_rhs(tril_ref[...], dta)
    cr = _dot_split_rhs(triu_ref[...], dta)
    fwd = lane < SSD_HEADS
    acs = jnp.where(fwd, cf, cr)
    tot = jnp.where(fwd[0:1], cf[t - 1:t, :], cr[0:1, :])
    return dt, acs, tot


def _ssd_compact(full):
    half = BRANCH_WIDTH // SSD_GROUPS
    return jnp.concatenate([full[:SSD_STATE, :half], full[SSD_STATE:, half:]], axis=1)


def _ssd_expand(compact):
    half = BRANCH_WIDTH // SSD_GROUPS
    z = jnp.zeros((SSD_STATE, half), compact.dtype)
    return jnp.concatenate([jnp.concatenate([compact[:, :half], z], axis=1),
                            jnp.concatenate([z, compact[:, half:]], axis=1)], axis=0)


def _ssd_local_stages(xbc_ref, xprev_ref, xnext_ref, dt_ref, cw_ref, cb_ref, alog_ref, dtb_ref, dexp_ref,
                      tril_ref, triu_ref, ef_ref, eb_ref, shift_ref,
                      yp_ref, cs_ref, e_ref, sf_ref, dec_ref, sb_ref, xe_ref, s_ref):
    i = pl.program_id(1)
    n = pl.num_programs(1)
    nci = xbc_ref.shape[0] // SSD_CHUNK
    t = SSD_CHUNK

    @pl.when(i == 0)
    def _():
        s_ref[...] = jnp.zeros_like(s_ref)

    _fill_halo_buffer(xe_ref, xbc_ref, xprev_ref, xnext_ref, i == n - 1, i == 0)

    lane = lax.broadcasted_iota(jnp.int32, (t, LANE), 1)
    low = lane < SSD_STATE
    row = lax.broadcasted_iota(jnp.int32, (t, t), 0)
    col = lax.broadcasted_iota(jnp.int32, (t, t), 1)
    tge = row >= col
    tle = row <= col
    hpg = SSD_HEADS // SSD_GROUPS

    chunks = range(nci)
    xc = []
    for c in chunks:
        xc.append(_ssd_conv(xe_ref, c * t, shift_ref, cw_ref, cb_ref))
        yield
    xs = [a[:, :BRANCH_WIDTH] for a in xc]
    bs = [a[:, BRANCH_WIDTH:BRANCH_WIDTH + LANE].astype(BF16) for a in xc]
    cs = [a[:, BRANCH_WIDTH + LANE:] for a in xc]
    terms = [_ssd_decay_terms(dt_ref[pl.ds(c * t, t), :], dtb_ref, alog_ref, tril_ref, triu_ref) for c in chunks]
    dt = [a[0] for a in terms]
    acs = [a[1] for a in terms]
    tot = [a[2] for a in terms]
    yield
    cb = [(_dot_nt(jnp.where(low, cs[c], 0.0).astype(BF16), bs[c]),
           _dot_nt(jnp.where(low, 0.0, cs[c]).astype(BF16), bs[c])) for c in chunks]
    acs_t = [a.T for a in acs]
    dt_t = [a.T for a in dt]
    yield
    w = [(jnp.exp(tot[c] - acs[c]) * dt[c]).astype(BF16) for c in chunks]
    w_f = [_dot(w[c], ef_ref[...]) for c in chunks]
    w_b = [_dot(w[c], eb_ref[...]) for c in chunks]
    dec8 = [jnp.broadcast_to(jnp.exp(tot[c]), (8, LANE)) for c in chunks]
    dec_f = [_dot_split_lhs(dec8[c], ef_ref[...]) for c in chunks]
    dec_b = [_dot_split_lhs(dec8[c], eb_ref[...]) for c in chunks]
    yield

    y_pairs = [[] for _ in chunks]
    for pair in range(SSD_HEADS // 2):
        acc = [None for _ in chunks]
        for sub in range(2):
            h = 2 * pair + sub
            hb = SSD_HEADS + h
            half_mask = low if sub == 0 else jnp.logical_not(low)
            for c in chunks:
                arg = jnp.where(tge, acs[c][:, h:h + 1] - acs_t[c][h:h + 1, :],
                                acs[c][:, hb:hb + 1] - acs_t[c][hb:hb + 1, :])
                wgt = jnp.where(tge, dt_t[c][h:h + 1, :], 0.0) + jnp.where(tle, dt_t[c][hb:hb + 1, :], 0.0)
                m = (cb[c][h // hpg] * jnp.exp(arg) * wgt).astype(BF16)
                x_half = jnp.where(half_mask, xs[c][:, pair * LANE:(pair + 1) * LANE], 0.0).astype(BF16)
                part = _dot(m, x_half)
                acc[c] = part if acc[c] is None else acc[c] + part
            yield
        for c in chunks:
            y_pairs[c].append(acc[c])

    st_f = [_ssd_compact(_dot_tn(bs[c], (xs[c] * w_f[c]).astype(BF16))) for c in chunks]
    st_b = [_ssd_compact(_dot_tn(bs[c], (xs[c] * w_b[c]).astype(BF16))) for c in chunks]
    yield
    srow = lax.broadcasted_iota(jnp.int32, (8, BRANCH_WIDTH), 0)
    for c in chunks:
        rows = pl.ds(c * t, t)
        y = jnp.concatenate(y_pairs[c], axis=1) + xs[c] * dexp_ref[...]
        yp_ref[rows, :] = y.astype(yp_ref.dtype)
        cs_ref[rows, :] = cs[c].astype(cs_ref.dtype)
        e_ref[rows, :] = jnp.exp(acs[c]).astype(e_ref.dtype)
        dec_ref[c] = jnp.where(srow == 0, dec_f[c], dec_b[c])
        sf_ref[c] = st_f[c].astype(sf_ref.dtype)
        yield
    for c in reversed(chunks):
        sb_ref[c] = s_ref[...].astype(sb_ref.dtype)
        s_ref[...] = s_ref[...] * dec_b[c][0:1, :] + st_b[c]


def _ssd_final_stages(yp_ref, cs_ref, e_ref, z_ref, sf_ref, dec_ref, sb_ref, ng_ref, ef_ref, eb_ref, o_ref, s_ref):
    i = pl.program_id(1)
    nci = yp_ref.shape[0] // SSD_CHUNK
    t = SSD_CHUNK

    @pl.when(i == 0)
    def _():
        s_ref[...] = jnp.zeros_like(s_ref)

    half = BRANCH_WIDTH // SSD_GROUPS
    ng = ng_ref[...]
    chunks = range(nci)
    rows = [pl.ds(c * t, t) for c in chunks]
    s_f = []
    for c in chunks:
        s_f.append(s_ref[...].astype(BF16))
        s_ref[...] = s_ref[...] * dec_ref[c][0:1, :] + sf_ref[c].astype(F32)
    yield
    ex_f = [_dot(e_ref[rows[c], :], ef_ref[...]) for c in chunks]
    ex_b = [_dot(e_ref[rows[c], :], eb_ref[...]) for c in chunks]
    yield
    off_f = [_dot(cs_ref[rows[c], :], _ssd_expand(s_f[c])) for c in chunks]
    off_b = [_dot(cs_ref[rows[c], :], _ssd_expand(sb_ref[c])) for c in chunks]
    yield
    for c in chunks:
        y = yp_ref[rows[c], :].astype(F32) + ex_f[c] * off_f[c] + ex_b[c] * off_b[c]
        zf = z_ref[rows[c], :].astype(F32)
        y = y * (zf * _sigmoid(zf))
        y = jnp.concatenate([_rms(y[:, g * half:(g + 1) * half], ng[:, g * half:(g + 1) * half])
                             for g in range(SSD_GROUPS)], axis=1)
        o_ref[rows[c], :] = y.astype(o_ref.dtype)
        yield


def _ssd_constants():
    t = SSD_CHUNK
    r = np.arange(t)
    tril = (r[:, None] >= r[None, :]).astype(np.float32)
    triu = (r[:, None] <= r[None, :]).astype(np.float32)
    col = np.arange(BRANCH_WIDTH)
    ef = (r[:, None] == (col[None, :] // SSD_HEAD_DIM)).astype(np.float32)
    eb = (r[:, None] == (col[None, :] // SSD_HEAD_DIM) + SSD_HEADS).astype(np.float32)
    pad = SSD_CONV // 2
    shift = np.concatenate([_band_matrix([j - pad]) for j in range(SSD_CONV) if j != pad], axis=0)
    return [jnp.asarray(a, BF16) for a in (tril, triu, ef, eb, shift)]


def _pool_stages(p_ref, pprev_ref, pnext_ref, band_ref, w_ref, scale_ref, o_ref, pe_ref, seq_len):
    i = pl.program_id(1)
    n = pl.num_programs(1)
    tl = p_ref.shape[0]
    t = POOL_CHUNK
    _fill_halo_buffer(pe_ref, p_ref, pprev_ref, pnext_ref, i == n - 1, i == 0)
    tile_start = (n - 1 - i) * tl
    blocks = [(r0, g) for r0 in range(0, tl, t) for g in range(len(POOL_WINDOWS))]
    cols = lambda g: slice(g * POOL_GROUP_DIM, (g + 1) * POOL_GROUP_DIM)
    sums = []
    for r0, g in blocks:
        sums.append(_dot(band_ref[g], pe_ref[r0:r0 + HALO_WINDOW, cols(g)]))
        if g == len(POOL_WINDOWS) - 1:
            yield
    pooled = []
    for (r0, g), acc in zip(blocks, sums):
        left = POOL_WINDOWS[g] // 2
        right = POOL_WINDOWS[g] - 1 - left
        pos = tile_start + r0 + lax.broadcasted_iota(jnp.int32, (t, 1), 0)
        cnt = (jnp.minimum(pos + right, seq_len - 1) - jnp.maximum(pos - left, 0) + 1).astype(F32)
        pooled.append((acc / cnt - pe_ref[HALO + r0:HALO + r0 + t, cols(g)].astype(F32)).astype(BF16))
        if g == len(POOL_WINDOWS) - 1:
            yield
    mixed = []
    for (r0, g), pb in zip(blocks, pooled):
        mixed.append(_dot(pb, w_ref[g]))
        if g == len(POOL_WINDOWS) - 1:
            yield
    for (r0, g), y in zip(blocks, mixed):
        o_ref[r0:r0 + t, cols(g)] = (y * scale_ref[:, cols(g)]).astype(o_ref.dtype)
    yield


def _gla_decay_matrix(ecol, col):
    return jnp.concatenate(
        [jnp.broadcast_to(ecol[h * GLA_KEY_DIM:(h + 1) * GLA_KEY_DIM, col:col + 1], (GLA_KEY_DIM, GLA_VAL_DIM))
         for h in range(GLA_HEADS)], axis=1)


def _gla_block_diag(compact):
    z = jnp.zeros((GLA_KEY_DIM, GLA_VAL_DIM), compact.dtype)
    return jnp.concatenate(
        [jnp.concatenate([compact[:, h * GLA_VAL_DIM:(h + 1) * GLA_VAL_DIM] if g == h else z
                          for g in range(GLA_HEADS)], axis=1) for h in range(GLA_HEADS)], axis=0)


def _gla_ecol(tots):
    padded = jnp.concatenate([tots, jnp.zeros((LANE - tots.shape[0], GLA_QK), F32)], axis=0)
    return jnp.exp(padded.T)


def _gla_local_block(blk, q_ref, k_ref, v_ref, glr_ref, w2_ref, gb_ref, tril_ref, cmask_ref, oi_ref, qd_ref,
                     tots_ref, results):
    t = GLA_CHUNK
    nch = GLA_TILE_CHUNKS
    tile = GLA_TILE
    rows = pl.ds(blk * tile, tile)

    pre = _dot(glr_ref[rows, :].astype(BF16), w2_ref[...]) + gb_ref[...]
    yield
    gk = -_softplus(-pre) * (1.0 / GLA_GATE_NORMALIZER)
    bc = _dot(tril_ref[...], gk.astype(BF16))
    yield
    last = [bc[c * t + t - 1:c * t + t, :] for c in range(nch)]
    tot = jnp.concatenate([jnp.broadcast_to(row, (t, 2 * GLA_QK)) for row in last], axis=0)
    bc_f = bc[:, :GLA_QK]
    bc_b = bc[:, GLA_QK:]
    rb = tot[:, GLA_QK:] - bc_b + gk[:, GLA_QK:]
    qf = q_ref[rows, :].astype(F32) * (GLA_KEY_DIM ** -0.5)
    kf = k_ref[rows, :].astype(F32)
    v = v_ref[rows, :]
    qd_f = qf * jnp.exp(bc_f)
    qd_b = qf * jnp.exp(rb)
    ki_f = (kf * jnp.exp(-bc_f)).astype(BF16)
    ki_b = (kf * jnp.exp(-rb)).astype(BF16)
    ke_f = kf * jnp.exp(tot[:, :GLA_QK] - bc_f)
    ke_b = kf * jnp.exp(bc_b - gk[:, GLA_QK:])
    qd_ref[rows, :GLA_QK] = qd_f.astype(qd_ref.dtype)
    qd_ref[rows, GLA_QK:] = qd_b.astype(qd_ref.dtype)

    lane = lax.broadcasted_iota(jnp.int32, (tile, GLA_QK), 1)

    def stack_heads(qd):
        return jnp.concatenate(
            [jnp.where((lane >= h * GLA_KEY_DIM) & (lane < (h + 1) * GLA_KEY_DIM), qd, 0.0)
             for h in range(GLA_HEADS)], axis=0).astype(BF16)

    att_f = _dot_nt(stack_heads(qd_f), ki_f)
    att_b = _dot_nt(stack_heads(qd_b), ki_b)
    yield
    row = lax.broadcasted_iota(jnp.int32, (tile, tile), 0)
    col = lax.broadcasted_iota(jnp.int32, (tile, tile), 1)
    same = (row & -t) == (col & -t)
    m_f = same & (row >= col)
    m_b = same & (row <= col)
    for h in range(GLA_HEADS):
        hrows = slice(h * tile, (h + 1) * tile)
        att = (jnp.where(m_f, att_f[hrows, :], 0.0) + jnp.where(m_b, att_b[hrows, :], 0.0)).astype(BF16)
        oi_ref[rows, h * GLA_VAL_DIM:(h + 1) * GLA_VAL_DIM] = _dot(
            att, v[:, h * GLA_VAL_DIM:(h + 1) * GLA_VAL_DIM]).astype(oi_ref.dtype)
    yield
    ket_f = ke_f.T.astype(BF16)
    ket_b = ke_b.T.astype(BF16)
    cmask = cmask_ref[...]
    st = []
    for h in range(GLA_HEADS):
        vh = v[:, h * GLA_VAL_DIM:(h + 1) * GLA_VAL_DIM]
        v_exp = jnp.concatenate([vh] * nch, axis=1) * cmask
        lhs = jnp.concatenate([ket_f[h * GLA_KEY_DIM:(h + 1) * GLA_KEY_DIM, :],
                               ket_b[h * GLA_KEY_DIM:(h + 1) * GLA_KEY_DIM, :]], axis=0)
        st.append(_dot(lhs, v_exp))
    tots = jnp.concatenate([r[:, :GLA_QK] for r in last] + [r[:, GLA_QK:] for r in last], axis=0)
    tots_ref[blk] = tots
    results.append((blk, _gla_ecol(tots), st))
    yield


def _gla_local_stages(q_ref, k_ref, v_ref, glr_ref, w2_ref, gb_ref, tril_ref, cmask_ref,
                      oi_ref, qd_ref, sf_ref, tots_ref, sb_ref, s_ref):
    i = pl.program_id(1)
    nch = GLA_TILE_CHUNKS
    n_blk = q_ref.shape[0] // GLA_TILE

    @pl.when(i == 0)
    def _():
        s_ref[...] = jnp.zeros_like(s_ref)

    results = []
    live = [_gla_local_block(blk, q_ref, k_ref, v_ref, glr_ref, w2_ref, gb_ref, tril_ref, cmask_ref, oi_ref,
                             qd_ref, tots_ref, results) for blk in range(n_blk)]
    while live:
        live = [g for g in live if next(g, StopIteration) is not StopIteration]
        yield

    for blk, ecol, st in sorted(results, key=lambda r: -r[0]):
        def local_state(c, lo):
            return jnp.concatenate([st[h][lo:lo + GLA_KEY_DIM, c * GLA_VAL_DIM:(c + 1) * GLA_VAL_DIM]
                                    for h in range(GLA_HEADS)], axis=1)

        for c in reversed(range(nch)):
            sf_ref[blk * nch + c] = local_state(c, 0).astype(sf_ref.dtype)
            sb_ref[blk * nch + c] = s_ref[...].astype(sb_ref.dtype)
            s_ref[...] = s_ref[...] * _gla_decay_matrix(ecol, nch + c) + local_state(c, GLA_KEY_DIM)
        yield


def _gla_final_stages(oi_ref, qd_ref, sf_ref, tots_ref, sb_ref, r_ref, ng_ref, o_ref, s_ref):
    i = pl.program_id(1)
    t = GLA_CHUNK
    nch = GLA_TILE_CHUNKS
    n_blk = oi_ref.shape[0] // GLA_TILE

    @pl.when(i == 0)
    def _():
        s_ref[...] = jnp.zeros_like(s_ref)

    ng = ng_ref[...]
    ecol = [_gla_ecol(tots_ref[blk]) for blk in range(n_blk)]
    chunks = range(n_blk * nch)
    rows = [pl.ds(c * t, t) for c in chunks]
    s_f = []
    for c in chunks:
        s_f.append(s_ref[...].astype(BF16))
        s_ref[...] = s_ref[...] * _gla_decay_matrix(ecol[c // nch], c % nch) + sf_ref[c].astype(F32)
    yield
    inter_f = [_dot(qd_ref[rows[c], :GLA_QK], _gla_block_diag(s_f[c])) for c in chunks]
    yield
    inter_b = [_dot(qd_ref[rows[c], GLA_QK:], _gla_block_diag(sb_ref[c])) for c in chunks]
    yield
    for blk in range(n_blk):
        brows = pl.ds(blk * GLA_TILE, GLA_TILE)
        o = jnp.concatenate([oi_ref[rows[c], :].astype(F32) + inter_f[c] + inter_b[c]
                             for c in range(blk * nch, (blk + 1) * nch)], axis=0)
        rf = r_ref[brows, :].astype(F32)
        o = jnp.concatenate([_rms(o[:, h * GLA_VAL_DIM:(h + 1) * GLA_VAL_DIM], ng[:, h * GLA_VAL_DIM:(h + 1) * GLA_VAL_DIM])
                             for h in range(GLA_HEADS)], axis=1)
        o_ref[brows, :] = (o * (rf * _sigmoid(rf))).astype(o_ref.dtype)
        yield


def _gla_constants():
    r = np.arange(GLA_TILE)
    tril = ((r[:, None] >= r[None, :]) & (r[:, None] // GLA_CHUNK == r[None, :] // GLA_CHUNK)).astype(np.float32)
    ccol = np.arange(GLA_TILE_CHUNKS * GLA_VAL_DIM) // GLA_VAL_DIM
    cmask = (r[:, None] // GLA_CHUNK == ccol[None, :]).astype(np.float32)
    return jnp.asarray(tril, BF16), jnp.asarray(cmask, BF16)


def _local_kernel(xbc_ref, xprev_ref, xnext_ref, dt_ref, q_ref, k_ref, v_ref, glr_ref, pin_ref, pprev_ref, pnext_ref,
                  cw_ref, cb_ref, alog_ref, dtb_ref, dexp_ref, w2_ref, gb_ref, pw_ref, pscale_ref,
                  stril_ref, striu_ref, ef_ref, eb_ref, shift_ref, gtril_ref, cmask_ref, band_ref,
                  yp_ref, cs_ref, e_ref, ssf_ref, dec_ref, ssb_ref, oi_ref, qd_ref, gsf_ref, tots_ref, gsb_ref,
                  pout_ref, xe_ref, pe_ref, s_ssd_ref, s_gla_ref, *, seq_len):
    _run_staged([
        _ssd_local_stages(xbc_ref, xprev_ref, xnext_ref, dt_ref, cw_ref, cb_ref, alog_ref, dtb_ref, dexp_ref,
                          stril_ref, striu_ref, ef_ref, eb_ref, shift_ref,
                          yp_ref, cs_ref, e_ref, ssf_ref, dec_ref, ssb_ref, xe_ref, s_ssd_ref),
        _gla_local_stages(q_ref, k_ref, v_ref, glr_ref, w2_ref, gb_ref, gtril_ref, cmask_ref,
                          oi_ref, qd_ref, gsf_ref, tots_ref, gsb_ref, s_gla_ref),
        _pool_stages(pin_ref, pprev_ref, pnext_ref, band_ref, pw_ref, pscale_ref, pout_ref, pe_ref, seq_len),
    ])


def _final_kernel(yp_ref, cs_ref, e_ref, z_ref, ssf_ref, dec_ref, ssb_ref, oi_ref, qd_ref, gsf_ref, tots_ref, gsb_ref,
                  r_ref, sng_ref, gng_ref, ef_ref, eb_ref, ssd_out_ref, gla_out_ref, s_ssd_ref, s_gla_ref):
    _run_staged([
        _ssd_final_stages(yp_ref, cs_ref, e_ref, z_ref, ssf_ref, dec_ref, ssb_ref, sng_ref, ef_ref, eb_ref,
                          ssd_out_ref, s_ssd_ref),
        _gla_final_stages(oi_ref, qd_ref, gsf_ref, tots_ref, gsb_ref, r_ref, gng_ref, gla_out_ref, s_gla_ref),
    ])


def _mixers(proj, p, layer, b, l, tl):
    seq = lambda name: proj[name].reshape(b, l, -1)
    n_tiles = l // tl
    nci = tl // SSD_CHUNK
    n_blk = tl // GLA_TILE
    nch = n_blk * GLA_TILE_CHUNKS
    stril, striu, ef, eb, shift = _ssd_constants()
    gtril, cmask = _gla_constants()
    band = jnp.asarray(np.stack([_band_matrix(range(-(w // 2), w - w // 2)) for w in POOL_WINDOWS]), BF16)
    xprev, xnext = _halo_specs(tl, SSD_CONV_CH, n_tiles, reverse=True)
    pprev, pnext = _halo_specs(tl, BRANCH_WIDTH, n_tiles, reverse=True)
    rev = lambda width: _tile_spec(tl, width, n_tiles)
    fwd = lambda width: _tile_spec(tl, width)
    act = lambda width: jax.ShapeDtypeStruct((b, l, width), BF16)
    ssd_state = jax.ShapeDtypeStruct((b, l // SSD_CHUNK, SSD_STATE, BRANCH_WIDTH), BF16)
    gla_state = jax.ShapeDtypeStruct((b, l // GLA_CHUNK, GLA_KEY_DIM, BRANCH_WIDTH), BF16)
    xbc, pin = seq("xbc"), seq("pool")
    yp, cs, e, ssf, dec, ssb, oi, qd, gsf, tots, gsb, pool_out = pl.pallas_call(
        functools.partial(_local_kernel, seq_len=l),
        grid=(b, n_tiles),
        in_specs=[rev(SSD_CONV_CH), xprev, xnext, rev(LANE), rev(GLA_QK), rev(GLA_QK), rev(BRANCH_WIDTH), rev(LANE),
                  rev(BRANCH_WIDTH), pprev, pnext,
                  _wspec(p["ssd_conv_w"], layer), _wspec(p["ssd_conv_b"], layer), _wspec(p["ssd_a_log"], layer),
                  _wspec(p["ssd_dt_bias"], layer), _wspec(p["ssd_d"], layer), _wspec(p["gla_w2"], layer),
                  _wspec(p["gla_gate_b"], layer), _wspec(p["pool_w"], layer), _wspec(p["pool_scale"], layer),
                  _cspec(stril), _cspec(striu), _cspec(ef), _cspec(eb), _cspec(shift), _cspec(gtril), _cspec(cmask),
                  _cspec(band)],
        out_specs=[rev(BRANCH_WIDTH), rev(LANE), rev(LANE), _chunk_spec(nci, SSD_STATE, BRANCH_WIDTH, n_tiles),
                   _chunk_spec(nci, 8, BRANCH_WIDTH, n_tiles), _chunk_spec(nci, SSD_STATE, BRANCH_WIDTH, n_tiles),
                   rev(BRANCH_WIDTH), rev(2 * GLA_QK), _chunk_spec(nch, GLA_KEY_DIM, BRANCH_WIDTH, n_tiles),
                   _chunk_spec(n_blk, 8, GLA_QK, n_tiles), _chunk_spec(nch, GLA_KEY_DIM, BRANCH_WIDTH, n_tiles),
                   rev(BRANCH_WIDTH)],
        out_shape=[act(BRANCH_WIDTH), act(LANE), act(LANE), ssd_state,
                   jax.ShapeDtypeStruct((b, l // SSD_CHUNK, 8, BRANCH_WIDTH), F32), ssd_state,
                   act(BRANCH_WIDTH), act(2 * GLA_QK), gla_state,
                   jax.ShapeDtypeStruct((b, l // GLA_TILE, 8, GLA_QK), F32), gla_state, act(BRANCH_WIDTH)],
        scratch_shapes=[pltpu.VMEM((tl + 2 * HALO, SSD_CONV_CH), BF16), pltpu.VMEM((tl + 2 * HALO, BRANCH_WIDTH), BF16),
                        pltpu.VMEM((SSD_STATE, BRANCH_WIDTH), F32), pltpu.VMEM((GLA_KEY_DIM, BRANCH_WIDTH), F32)],
        compiler_params=_params(2),
        name="mix_local",
    )(xbc, xbc, xbc, seq("dtg"), seq("q"), seq("k"), seq("gv"), seq("dtg"), pin, pin, pin,
      p["ssd_conv_w"], p["ssd_conv_b"], p["ssd_a_log"], p["ssd_dt_bias"], p["ssd_d"], p["gla_w2"], p["gla_gate_b"],
      p["pool_w"], p["pool_scale"], stril, striu, ef, eb, shift, gtril, cmask, band)

    ssd_out, gla_out = pl.pallas_call(
        _final_kernel,
        grid=(b, n_tiles),
        in_specs=[fwd(BRANCH_WIDTH), fwd(LANE), fwd(LANE), fwd(BRANCH_WIDTH), _chunk_spec(nci, SSD_STATE, BRANCH_WIDTH),
                  _chunk_spec(nci, 8, BRANCH_WIDTH), _chunk_spec(nci, SSD_STATE, BRANCH_WIDTH),
                  fwd(BRANCH_WIDTH), fwd(2 * GLA_QK), _chunk_spec(nch, GLA_KEY_DIM, BRANCH_WIDTH),
                  _chunk_spec(n_blk, 8, GLA_QK), _chunk_spec(nch, GLA_KEY_DIM, BRANCH_WIDTH), fwd(BRANCH_WIDTH),
                  _wspec(p["ssd_norm"], layer), _wspec(p["gla_norm"], layer), _cspec(ef), _cspec(eb)],
        out_specs=[fwd(BRANCH_WIDTH), fwd(BRANCH_WIDTH)],
        out_shape=[act(BRANCH_WIDTH), act(BRANCH_WIDTH)],
        scratch_shapes=[pltpu.VMEM((SSD_STATE, BRANCH_WIDTH), F32), pltpu.VMEM((GLA_KEY_DIM, BRANCH_WIDTH), F32)],
        compiler_params=_params(2),
        name="mix_final",
    )(yp, cs, e, seq("z"), ssf, dec, ssb, oi, qd, gsf, tots, gsb, seq("r"), p["ssd_norm"], p["gla_norm"], ef, eb)
    return ssd_out, pool_out, gla_out


def _sgu_stages(u_ref, v_ref, ng_ref, w_ref, bias_ref, out_ref, slot):
    gd = BRANCH_WIDTH // SGU_GROUPS
    for r in range(0, u_ref.shape[0], SGU_CHUNK):
        vf = _rms(_gelu_tanh(v_ref[r:r + SGU_CHUNK, :].astype(F32)), ng_ref[...]).astype(BF16)
        yield
        mixed = [_dot(w_ref[g], vf[:, g * gd:(g + 1) * gd]) for g in range(SGU_GROUPS)]
        uf = _gelu_tanh(u_ref[r:r + SGU_CHUNK, :].astype(F32))
        yield
        out_ref[slot, r:r + SGU_CHUNK, :] = (uf * (jnp.concatenate(mixed, axis=1) + bias_ref[...])).astype(BF16)
        yield


def _merge_ffn_stages(x_ref, branches, gate_ref, wb_ref, wo_ref, gpost_ref, gpre_ref, w1_ref, w2_ref, gffn_ref,
                      o_ref):
    merged = None
    for n, load in branches:
        term = gate_ref[:, n * D_MODEL:(n + 1) * D_MODEL].astype(F32) * _dot(load(), wb_ref[n])
        merged = term if merged is None else merged + term
        yield
    x1 = x_ref[...] + _rms(_dot(merged.astype(BF16), wo_ref[...]), gpost_ref[...])
    h2 = _rms(x1, gpre_ref[...]).astype(BF16)
    yield
    ff = None
    for c in range(0, D_FF, D_MODEL):
        hid = jnp.square(jnp.maximum(_dot(h2, w1_ref[:, c:c + D_MODEL]), 0.0)).astype(BF16)
        yield
        part = _dot(hid, w2_ref[c:c + D_MODEL, :])
        ff = part if ff is None else ff + part
        yield
    o_ref[...] = x1 + _rms(ff, gffn_ref[...])


def _merge_ffn_kernel(x_ref, b0_ref, b1_ref, u0_ref, v0_ref, un_ref, vn_ref, b3_ref, gate_ref, sng_ref, sw_ref,
                      sbias_ref, wb_ref, wo_ref, gpost_ref, gpre_ref, w1_ref, w2_ref, gffn_ref, o_ref, sgu_ref):
    i = pl.program_id(0)

    @pl.when(i == 0)
    def _():
        _run_staged([_sgu_stages(u0_ref, v0_ref, sng_ref, sw_ref, sbias_ref, sgu_ref, 0)])

    branches = ((0, lambda: b0_ref[...]), (1, lambda: b1_ref[...]), (3, lambda: b3_ref[...]),
                (2, lambda: sgu_ref[i % 2]))
    _run_staged([
        _merge_ffn_stages(x_ref, branches, gate_ref, wb_ref, wo_ref, gpost_ref, gpre_ref, w1_ref, w2_ref, gffn_ref,
                          o_ref),
        _sgu_stages(un_ref, vn_ref, sng_ref, sw_ref, sbias_ref, sgu_ref, (i + 1) % 2)])


def _merge_ffn(x2d, ssd_out, pool_out, sgu_u, sgu_v, gla_out, gate, p, layer, tm):
    n = x2d.shape[0]
    steps = n // tm
    row = lambda w: pl.BlockSpec((tm, w), lambda i: (i, 0))
    first = pl.BlockSpec((tm, BRANCH_WIDTH), lambda i: (0, 0), pipeline_mode=pl.Buffered(1))
    ahead = pl.BlockSpec((tm, BRANCH_WIDTH), lambda i: (jnp.minimum(i + 1, steps - 1), 0))
    return pl.pallas_call(
        _merge_ffn_kernel,
        grid=(steps,),
        in_specs=[row(D_MODEL), row(BRANCH_WIDTH), row(BRANCH_WIDTH), first, first, ahead, ahead, row(BRANCH_WIDTH),
                  row(N_BRANCH * D_MODEL),
                  _wspec(p["sgu_norm"], layer), _wspec(p["sgu_w"], layer), _wspec(p["sgu_bias"], layer),
                  _wspec(p["w_branch"], layer), _wspec(p["w_out"], layer), _wspec(p["norm_mix_post"], layer),
                  _wspec(p["norm_ffn_pre"], layer), _wspec(p["w_ff1"], layer), _wspec(p["w_ff2"], layer),
                  _wspec(p["norm_ffn_post"], layer)],
        out_specs=row(D_MODEL),
        out_shape=jax.ShapeDtypeStruct((n, D_MODEL), F32),
        scratch_shapes=[pltpu.VMEM((2, tm, BRANCH_WIDTH), BF16)],
        compiler_params=_params(1),
        name="merge_ffn",
    )(x2d, ssd_out, pool_out, sgu_u, sgu_v, sgu_u, sgu_v, gla_out, gate, p["sgu_norm"], p["sgu_w"], p["sgu_bias"],
      p["w_branch"], p["w_out"], p["norm_mix_post"], p["norm_ffn_pre"], p["w_ff1"], p["w_ff2"],
      p["norm_ffn_post"])


def _pad_lanes(a, width):
    return jnp.pad(a, [(0, 0)] * (a.ndim - 1) + [(0, width - a.shape[-1])])


def _prepare_params(norm_mix_pre, w_in, ssd_conv_w, ssd_conv_b, ssd_a_log, ssd_dt_bias, ssd_d, ssd_norm, pool_w,
                    pool_scale, sgu_norm, sgu_w, sgu_b, gla_gate_w2, gla_gate_b, gla_norm, w_branch, w_out,
                    norm_mix_post, norm_ffn_pre, w_ff1, w_ff2, norm_ffn_post):
    depth = w_in.shape[0]
    pieces = []
    for _, sources, padded, _ in PROJ_GROUPS:
        cols = [w_in[:, :, _REF_COLS[s][0]:_REF_COLS[s][0] + _REF_COLS[s][1]] for s in sources]
        pieces.append(_pad_lanes(jnp.concatenate(cols, axis=-1), padded))
    row = lambda a: a.reshape(depth, 1, -1).astype(F32)
    lo = SSD_DT_LANES
    w2 = jnp.zeros((depth, LANE, 2 * GLA_QK), F32)
    w2 = w2.at[:, lo:lo + GLA_GATE_RANK, :GLA_QK].set(gla_gate_w2[:, 0])
    w2 = w2.at[:, lo + GLA_GATE_RANK:lo + 2 * GLA_GATE_RANK, GLA_QK:].set(gla_gate_w2[:, 1])
    return {
        "norm_mix_pre": row(norm_mix_pre),
        "w_in": jnp.concatenate(pieces, axis=-1).astype(BF16),
        "ssd_conv_w": jnp.pad(ssd_conv_w.astype(F32), [(0, 0), (0, 8 - SSD_CONV), (0, 0)]),
        "ssd_conv_b": row(ssd_conv_b),
        "ssd_a_log": _pad_lanes(row(ssd_a_log), LANE),
        "ssd_dt_bias": _pad_lanes(row(ssd_dt_bias), LANE),
        "ssd_d": row(jnp.repeat(ssd_d, SSD_HEAD_DIM, axis=-1)),
        "ssd_norm": row(ssd_norm),
        "pool_w": pool_w.astype(BF16),
        "pool_scale": row(pool_scale),
        "sgu_norm": row(sgu_norm),
        "sgu_w": sgu_w.astype(BF16),
        "sgu_bias": jnp.repeat(jnp.swapaxes(sgu_b, 1, 2), BRANCH_WIDTH // SGU_GROUPS, axis=-1).astype(F32),
        "gla_w2": w2.astype(BF16),
        "gla_gate_b": row(gla_gate_b),
        "gla_norm": row(gla_norm),
        "w_branch": w_branch.astype(BF16),
        "w_out": w_out.astype(BF16),
        "norm_mix_post": row(norm_mix_post),
        "norm_ffn_pre": row(norm_ffn_pre),
        "w_ff1": w_ff1.astype(BF16),
        "w_ff2": w_ff2.astype(BF16),
        "norm_ffn_post": row(norm_ffn_post),
    }


def _tiles(seq_len):
    return min(512, seq_len), min(512, seq_len), min(1024, seq_len)


def _trunk(x, p):
    b, l, d = x.shape
    tm_proj, tm_ffn, tl = _tiles(l)
    x2d = x.reshape(b * l, d)
    for layer in range(DEPTH):
        proj = dict(zip([g[0] for g in PROJ_GROUPS], _in_proj(x2d, p["norm_mix_pre"], p["w_in"], layer, tm_proj)))
        flat = lambda a: a.reshape(b * l, BRANCH_WIDTH)
        ssd_out, pool_out, gla_out = _mixers(proj, p, layer, b, l, tl)
        x2d = _merge_ffn(x2d, flat(ssd_out), flat(pool_out), proj["u"], proj["v"], flat(gla_out), proj["gate"],
                         p, layer, tm_ffn)
    return x2d.reshape(b, l, d)


def kernel(x_prompt, x_sample, norm_mix_pre, w_in, ssd_conv_w, ssd_conv_b, ssd_a_log, ssd_dt_bias, ssd_d, ssd_norm, pool_w, pool_scale, sgu_norm, sgu_w, sgu_b, gla_gate_w2, gla_gate_b, gla_norm, w_branch, w_out, norm_mix_post, norm_ffn_pre, w_ff1, w_ff2, norm_ffn_post):
    p = _prepare_params(norm_mix_pre, w_in, ssd_conv_w, ssd_conv_b, ssd_a_log, ssd_dt_bias, ssd_d, ssd_norm, pool_w,
                        pool_scale, sgu_norm, sgu_w, sgu_b, gla_gate_w2, gla_gate_b, gla_norm, w_branch, w_out,
                        norm_mix_post, norm_ffn_pre, w_ff1, w_ff2, norm_ffn_post)
    return (_trunk(x_prompt, p), _trunk(x_sample, p))
```

```python
import functools

import jax
import jax.numpy as jnp
import numpy as np
from jax import lax
from jax.experimental import pallas as pl
from jax.experimental.pallas import tpu as pltpu

F32 = jnp.float32
BF16 = jnp.bfloat16

D_MODEL = 1024
DEPTH = 4
BRANCH_WIDTH = 512
N_BRANCH = 4
SSD_HEADS = 8
SSD_HEAD_DIM = 64
SSD_GROUPS = 2
SSD_STATE = 64
SSD_CONV = 5
SSD_CHUNK = 128
SSD_CONV_CH = 768
POOL_WINDOWS = (2, 4, 8, 16)
POOL_GROUP_DIM = 128
POOL_CHUNK = 128
SGU_CHUNK = 128
SGU_GROUPS = 4
GLA_HEADS = 4
GLA_KEY_DIM = 64
GLA_VAL_DIM = 128
GLA_GATE_RANK = 16
GLA_GATE_NORMALIZER = 16.0
GLA_CHUNK = 64
GLA_QK = GLA_HEADS * GLA_KEY_DIM
GLA_TILE = 256
GLA_TILE_CHUNKS = GLA_TILE // GLA_CHUNK
D_FF = 4096
RMS_EPS = 1e-6

LANE = 128
HALO = 16
VMEM_LIMIT = 56 * 1024 * 1024

_REF_COLS = {}
_off = 0
for _name, _width in (("z", 512), ("xbc", 768), ("dt", 16), ("pool", 512), ("u", 512), ("v", 512), ("q", 256),
                      ("k", 256), ("gv", 512), ("r", 512), ("glr", 32), ("gate", 4096)):
    _REF_COLS[_name] = (_off, _width)
    _off += _width
SSD_DT_LANES = 2 * SSD_HEADS
PROJ_GROUPS = (
    ("gate", ("gate",), 4096, BF16), ("z", ("z",), 512, BF16), ("xbc", ("xbc",), 768, BF16),
    ("dtg", ("dt", "glr"), LANE, F32), ("pool", ("pool",), 512, BF16), ("u", ("u",), 512, BF16),
    ("v", ("v",), 512, BF16), ("q", ("q",), 256, BF16), ("k", ("k",), 256, BF16), ("gv", ("gv",), 512, BF16),
    ("r", ("r",), 512, BF16),
)
PROJ_CHUNK = 512


LOG2_E = float(np.log2(np.e))


def _sigmoid(x):
    return 1.0 / (1.0 + jnp.exp2(x * (-LOG2_E)))


def _softplus(x):
    return jnp.maximum(x, 0.0) + jnp.log(1.0 + jnp.exp(-jnp.abs(x)))


def _gelu_tanh(x):
    c = -2.0 * np.sqrt(2.0 / np.pi) * LOG2_E
    return x / (1.0 + jnp.exp2(x * (c + (c * 0.044715) * (x * x))))


def _rms(x, g):
    return x * lax.rsqrt(jnp.mean(x * x, axis=-1, keepdims=True) + RMS_EPS) * g


def _dot(a, b):
    return jnp.dot(a, b, preferred_element_type=F32)


def _dot_tn(a, b):
    return lax.dot_general(a, b, (((0,), (0,)), ((), ())), preferred_element_type=F32)


def _dot_nt(a, b):
    return lax.dot_general(a, b, (((1,), (1,)), ((), ())), preferred_element_type=F32)


def _split_bf16(x):
    hi = x.astype(BF16)
    lo = (x - hi.astype(F32)).astype(BF16)
    return hi, lo


def _dot_split_rhs(a_bf16, x):
    hi, lo = _split_bf16(x)
    return _dot(a_bf16, hi) + _dot(a_bf16, lo)


def _dot_split_lhs(x, b_bf16):
    hi, lo = _split_bf16(x)
    return _dot(hi, b_bf16) + _dot(lo, b_bf16)


def _wspec(arr, layer):
    nd = arr.ndim - 1
    return pl.BlockSpec((None,) + tuple(arr.shape[1:]), lambda *_: (layer,) + (0,) * nd,
                        pipeline_mode=pl.Buffered(1))


def _cspec(arr):
    nd = arr.ndim
    return pl.BlockSpec(tuple(arr.shape), lambda *_: (0,) * nd, pipeline_mode=pl.Buffered(1))


def _params(n_grid):
    return pltpu.CompilerParams(dimension_semantics=("arbitrary",) * n_grid, vmem_limit_bytes=VMEM_LIMIT)


def _run_staged(generators):
    live = list(generators)
    while live:
        live = [g for g in live if next(g, StopIteration) is not StopIteration]


def _in_proj_kernel(x_ref, g_ref, w_ref, *out_refs):
    h = _rms(x_ref[...], g_ref[...]).astype(BF16)
    off = 0
    for o_ref, (name, _, width, dtype) in zip(out_refs, PROJ_GROUPS):
        for c in range(0, width, PROJ_CHUNK):
            cw = min(PROJ_CHUNK, width - c)
            acc = _dot(h, w_ref[:, off + c:off + c + cw])
            if name == "gate":
                acc = _sigmoid(acc)
            o_ref[:, c:c + cw] = acc.astype(dtype)
        off += width


def _in_proj(x2d, norm_g, w_in_p, layer, tm):
    n = x2d.shape[0]
    out_shape = [jax.ShapeDtypeStruct((n, g[2]), g[3]) for g in PROJ_GROUPS]
    out_specs = [pl.BlockSpec((tm, g[2]), lambda i: (i, 0)) for g in PROJ_GROUPS]
    return pl.pallas_call(
        _in_proj_kernel,
        grid=(n // tm,),
        in_specs=[pl.BlockSpec((tm, D_MODEL), lambda i: (i, 0)), _wspec(norm_g, layer), _wspec(w_in_p, layer)],
        out_specs=out_specs,
        out_shape=out_shape,
        compiler_params=_params(1),
        name="in_proj",
    )(x2d, norm_g, w_in_p)


def _tile_spec(tl, width, n_tiles=None):
    if n_tiles is None:
        return pl.BlockSpec((None, tl, width), lambda b, i: (b, i, 0))
    return pl.BlockSpec((None, tl, width), lambda b, i: (b, n_tiles - 1 - i, 0))


def _chunk_spec(per_tile, rows, width, n_tiles=None):
    if n_tiles is None:
        return pl.BlockSpec((None, per_tile, rows, width), lambda b, i: (b, i, 0, 0))
    return pl.BlockSpec((None, per_tile, rows, width), lambda b, i: (b, n_tiles - 1 - i, 0, 0))


def _halo_specs(tl, width, n_tiles, reverse=False):
    per = tl // HALO
    last = n_tiles * per - 1

    def tile(i):
        return n_tiles - 1 - i if reverse else i

    prev = pl.BlockSpec((None, HALO, width), lambda b, i: (b, jnp.maximum(tile(i) * per - 1, 0), 0))
    nxt = pl.BlockSpec((None, HALO, width), lambda b, i: (b, jnp.minimum((tile(i) + 1) * per, last), 0))
    return prev, nxt


def _fill_halo_buffer(ext_ref, cur_ref, prev_ref, next_ref, is_first, is_last):
    tl = cur_ref.shape[0]
    zeros = jnp.zeros((HALO, ext_ref.shape[1]), ext_ref.dtype)
    ext_ref[0:HALO, :] = jnp.where(is_first, zeros, prev_ref[...])
    ext_ref[HALO:HALO + tl, :] = cur_ref[...]
    ext_ref[HALO + tl:2 * HALO + tl, :] = jnp.where(is_last, zeros, next_ref[...])


HALO_WINDOW = 128 + 2 * HALO


def _band_matrix(offsets):
    r = np.arange(128)[:, None]
    w = np.arange(HALO_WINDOW)[None, :]
    return sum((w == r + HALO + off).astype(np.float32) for off in offsets)


def _ssd_conv(xe_ref, r0, shift_ref, cw_ref, cb_ref):
    t = SSD_CHUNK
    pad = SSD_CONV // 2
    win = xe_ref[pl.ds(r0, HALO_WINDOW), :]
    shifted = _dot(shift_ref[...], win)
    acc = cb_ref[...] + cw_ref[pad:pad + 1, :] * win[HALO:HALO + t, :].astype(F32)
    for n, j in enumerate(j for j in range(SSD_CONV) if j != pad):
        acc = acc + cw_ref[j:j + 1, :] * shifted[n * t:(n + 1) * t, :]
    return acc * _sigmoid(acc)


def _ssd_decay_terms(dt_raw, dtb_ref, alog_ref, tril_ref, triu_ref):
    t = dt_raw.shape[0]
    lane = lax.broadcasted_iota(jnp.int32, (t, LANE), 1)
    used = lane < 2 * SSD_HEADS
    dt = jnp.where(used, _softplus(dt_raw + dtb_ref[...]), 0.0)
    a = jnp.where(used[0:1], -jnp.exp(alog_ref[...]), 0.0)
    dta = dt * a
    cf = _dot_split_rhs(tril_ref[...], dta)
    cr = _dot_split_rhs(triu_ref[...], dta)
    fwd = lane < SSD_HEADS
    acs = jnp.where(fwd, cf, cr)
    tot = jnp.where(fwd[0:1], cf[t - 1:t, :], cr[0:1, :])
    return dt, acs, tot


def _ssd_compact(full):
    half = BRANCH_WIDTH // SSD_GROUPS
    return jnp.concatenate([full[:SSD_STATE, :half], full[SSD_STATE:, half:]], axis=1)


def _ssd_expand(compact):
    half = BRANCH_WIDTH // SSD_GROUPS
    z = jnp.zeros((SSD_STATE, half), compact.dtype)
    return jnp.concatenate([jnp.concatenate([compact[:, :half], z], axis=1),
                            jnp.concatenate([z, compact[:, half:]], axis=1)], axis=0)


def _ssd_local_stages(xbc_ref, xprev_ref, xnext_ref, dt_ref, cw_ref, cb_ref, alog_ref, dtb_ref, dexp_ref,
                      tril_ref, triu_ref, ef_ref, eb_ref, shift_ref,
                      yp_ref, cs_ref, e_ref, sf_ref, dec_ref, sb_ref, xe_ref, s_ref):
    i = pl.program_id(1)
    n = pl.num_programs(1)
    nci = xbc_ref.shape[0] // SSD_CHUNK
    t = SSD_CHUNK

    @pl.when(i == 0)
    def _():
        s_ref[...] = jnp.zeros_like(s_ref)

    _fill_halo_buffer(xe_ref, xbc_ref, xprev_ref, xnext_ref, i == n - 1, i == 0)

    lane = lax.broadcasted_iota(jnp.int32, (t, LANE), 1)
    low = lane < SSD_STATE
    row = lax.broadcasted_iota(jnp.int32, (t, t), 0)
    col = lax.broadcasted_iota(jnp.int32, (t, t), 1)
    tge = row >= col
    teq = row == col
    hpg = SSD_HEADS // SSD_GROUPS

    chunks = range(nci)
    xc = []
    for c in chunks:
        xc.append(_ssd_conv(xe_ref, c * t, shift_ref, cw_ref, cb_ref))
        yield
    xs = [a[:, :BRANCH_WIDTH] for a in xc]
    bs = [a[:, BRANCH_WIDTH:BRANCH_WIDTH + LANE].astype(BF16) for a in xc]
    cs = [a[:, BRANCH_WIDTH + LANE:] for a in xc]
    terms = [_ssd_decay_terms(dt_ref[pl.ds(c * t, t), :], dtb_ref, alog_ref, tril_ref, triu_ref) for c in chunks]
    dt = [a[0] for a in terms]
    acs = [a[1] for a in terms]
    tot = [a[2] for a in terms]
    yield
    cb = [(_dot_nt(jnp.where(low, cs[c], 0.0).astype(BF16), bs[c]),
           _dot_nt(jnp.where(low, 0.0, cs[c]).astype(BF16), bs[c])) for c in chunks]
    dt_t = [a.T for a in dt]
    src_t = [acs[c].T - jnp.log(dt_t[c]) for c in chunks]
    yield
    w = [(jnp.exp(tot[c] - acs[c]) * dt[c]).astype(BF16) for c in chunks]
    w_f = [_dot(w[c], ef_ref[...]) for c in chunks]
    w_b = [_dot(w[c], eb_ref[...]) for c in chunks]
    dec8 = [jnp.broadcast_to(jnp.exp(tot[c]), (8, LANE)) for c in chunks]
    dec_f = [_dot_split_lhs(dec8[c], ef_ref[...]) for c in chunks]
    dec_b = [_dot_split_lhs(dec8[c], eb_ref[...]) for c in chunks]
    yield

    y_pairs = [[] for _ in chunks]
    for pair in range(SSD_HEADS // 2):
        acc = [None for _ in chunks]
        for sub in range(2):
            h = 2 * pair + sub
            hb = SSD_HEADS + h
            half_mask = low if sub == 0 else jnp.logical_not(low)
            for c in chunks:
                arg = jnp.where(tge, acs[c][:, h:h + 1] - src_t[c][h:h + 1, :],
                                acs[c][:, hb:hb + 1] - src_t[c][hb:hb + 1, :])
                decay = jnp.exp(arg) + jnp.where(teq, dt_t[c][hb:hb + 1, :], 0.0)
                m = (cb[c][h // hpg] * decay).astype(BF16)
                x_half = jnp.where(half_mask, xs[c][:, pair * LANE:(pair + 1) * LANE], 0.0).astype(BF16)
                part = _dot(m, x_half)
                acc[c] = part if acc[c] is None else acc[c] + part
            yield
        for c in chunks:
            y_pairs[c].append(acc[c])

    st_f = [_ssd_compact(_dot_tn(bs[c], (xs[c] * w_f[c]).astype(BF16))) for c in chunks]
    st_b = [_ssd_compact(_dot_tn(bs[c], (xs[c] * w_b[c]).astype(BF16))) for c in chunks]
    yield
    srow = lax.broadcasted_iota(jnp.int32, (8, BRANCH_WIDTH), 0)
    for c in chunks:
        rows = pl.ds(c * t, t)
        y = jnp.concatenate(y_pairs[c], axis=1) + xs[c] * dexp_ref[...]
        yp_ref[rows, :] = y.astype(yp_ref.dtype)
        cs_ref[rows, :] = cs[c].astype(cs_ref.dtype)
        e_ref[rows, :] = jnp.exp(acs[c]).astype(e_ref.dtype)
        dec_ref[c] = jnp.where(srow == 0, dec_f[c], dec_b[c])
        sf_ref[c] = st_f[c].astype(sf_ref.dtype)
        yield
    for c in reversed(chunks):
        sb_ref[c] = s_ref[...].astype(sb_ref.dtype)
        s_ref[...] = s_ref[...] * dec_b[c][0:1, :] + st_b[c]


def _ssd_final_stages(yp_ref, cs_ref, e_ref, z_ref, sf_ref, dec_ref, sb_ref, ng_ref, ef_ref, eb_ref, o_ref, s_ref):
    i = pl.program_id(1)
    nci = yp_ref.shape[0] // SSD_CHUNK
    t = SSD_CHUNK

    @pl.when(i == 0)
    def _():
        s_ref[...] = jnp.zeros_like(s_ref)

    half = BRANCH_WIDTH // SSD_GROUPS
    ng = ng_ref[...]
    chunks = range(nci)
    rows = [pl.ds(c * t, t) for c in chunks]
    s_f = []
    for c in chunks:
        s_f.append(s_ref[...].astype(BF16))
        s_ref[...] = s_ref[...] * dec_ref[c][0:1, :] + sf_ref[c].astype(F32)
    yield
    ex_f = [_dot(e_ref[rows[c], :], ef_ref[...]) for c in chunks]
    ex_b = [_dot(e_ref[rows[c], :], eb_ref[...]) for c in chunks]
    yield
    off_f = [_dot(cs_ref[rows[c], :], _ssd_expand(s_f[c])) for c in chunks]
    off_b = [_dot(cs_ref[rows[c], :], _ssd_expand(sb_ref[c])) for c in chunks]
    yield
    for c in chunks:
        y = yp_ref[rows[c], :].astype(F32) + ex_f[c] * off_f[c] + ex_b[c] * off_b[c]
        zf = z_ref[rows[c], :].astype(F32)
        y = y * (zf * _sigmoid(zf))
        y = jnp.concatenate([_rms(y[:, g * half:(g + 1) * half], ng[:, g * half:(g + 1) * half])
                             for g in range(SSD_GROUPS)], axis=1)
        o_ref[rows[c], :] = y.astype(o_ref.dtype)
        yield


def _ssd_constants():
    t = SSD_CHUNK
    r = np.arange(t)
    tril = (r[:, None] >= r[None, :]).astype(np.float32)
    triu = (r[:, None] <= r[None, :]).astype(np.float32)
    col = np.arange(BRANCH_WIDTH)
    ef = (r[:, None] == (col[None, :] // SSD_HEAD_DIM)).astype(np.float32)
    eb = (r[:, None] == (col[None, :] // SSD_HEAD_DIM) + SSD_HEADS).astype(np.float32)
    pad = SSD_CONV // 2
    shift = np.concatenate([_band_matrix([j - pad]) for j in range(SSD_CONV) if j != pad], axis=0)
    return [jnp.asarray(a, BF16) for a in (tril, triu, ef, eb, shift)]


def _pool_stages(p_ref, pprev_ref, pnext_ref, band_ref, w_ref, scale_ref, o_ref, pe_ref, seq_len):
    i = pl.program_id(1)
    n = pl.num_programs(1)
    tl = p_ref.shape[0]
    t = POOL_CHUNK
    _fill_halo_buffer(pe_ref, p_ref, pprev_ref, pnext_ref, i == 0, i == n - 1)
    tile_start = i * tl
    blocks = [(r0, g) for r0 in range(0, tl, t) for g in range(len(POOL_WINDOWS))]
    cols = lambda g: slice(g * POOL_GROUP_DIM, (g + 1) * POOL_GROUP_DIM)
    sums = []
    for r0, g in blocks:
        sums.append(_dot(band_ref[g], pe_ref[r0:r0 + HALO_WINDOW, cols(g)]))
        if g == len(POOL_WINDOWS) - 1:
            yield
    pooled = []
    for (r0, g), acc in zip(blocks, sums):
        left = POOL_WINDOWS[g] // 2
        right = POOL_WINDOWS[g] - 1 - left
        pos = tile_start + r0 + lax.broadcasted_iota(jnp.int32, (t, 1), 0)
        cnt = (jnp.minimum(pos + right, seq_len - 1) - jnp.maximum(pos - left, 0) + 1).astype(F32)
        pooled.append((acc / cnt - pe_ref[HALO + r0:HALO + r0 + t, cols(g)].astype(F32)).astype(BF16))
        if g == len(POOL_WINDOWS) - 1:
            yield
    mixed = []
    for (r0, g), pb in zip(blocks, pooled):
        mixed.append(_dot(pb, w_ref[g]))
        if g == len(POOL_WINDOWS) - 1:
            yield
    for (r0, g), y in zip(blocks, mixed):
        o_ref[r0:r0 + t, cols(g)] = (y * scale_ref[:, cols(g)]).astype(o_ref.dtype)
    yield


def _gla_decay_matrix(ecol, col):
    return jnp.concatenate(
        [jnp.broadcast_to(ecol[h * GLA_KEY_DIM:(h + 1) * GLA_KEY_DIM, col:col + 1], (GLA_KEY_DIM, GLA_VAL_DIM))
         for h in range(GLA_HEADS)], axis=1)


def _gla_block_diag(compact):
    z = jnp.zeros((GLA_KEY_DIM, GLA_VAL_DIM), compact.dtype)
    return jnp.concatenate(
        [jnp.concatenate([compact[:, h * GLA_VAL_DIM:(h + 1) * GLA_VAL_DIM] if g == h else z
                          for g in range(GLA_HEADS)], axis=1) for h in range(GLA_HEADS)], axis=0)


def _gla_ecol(tots):
    padded = jnp.concatenate([tots, jnp.zeros((LANE - tots.shape[0], GLA_QK), F32)], axis=0)
    return jnp.exp(padded.T)


def _gla_local_block(blk, q_ref, k_ref, v_ref, glr_ref, w2_ref, gb_ref, tril_ref, cmask_ref, oi_ref, qd_ref,
                     tots_ref, results):
    t = GLA_CHUNK
    nch = GLA_TILE_CHUNKS
    tile = GLA_TILE
    rows = pl.ds(blk * tile, tile)

    pre = _dot(glr_ref[rows, :].astype(BF16), w2_ref[...]) + gb_ref[...]
    yield
    gk = -_softplus(-pre) * (1.0 / GLA_GATE_NORMALIZER)
    bc = _dot(tril_ref[...], gk.astype(BF16))
    yield
    last = [bc[c * t + t - 1:c * t + t, :] for c in range(nch)]
    tot = jnp.concatenate([jnp.broadcast_to(row, (t, 2 * GLA_QK)) for row in last], axis=0)
    bc_f = bc[:, :GLA_QK]
    bc_b = bc[:, GLA_QK:]
    rb = tot[:, GLA_QK:] - bc_b + gk[:, GLA_QK:]
    qf = q_ref[rows, :].astype(F32) * (GLA_KEY_DIM ** -0.5)
    kf = k_ref[rows, :].astype(F32)
    v = v_ref[rows, :]
    qd_f = qf * jnp.exp(bc_f)
    qd_b = qf * jnp.exp(rb)
    ki_f = (kf * jnp.exp(-bc_f)).astype(BF16)
    ki_b = (kf * jnp.exp(-rb)).astype(BF16)
    ke_f = kf * jnp.exp(tot[:, :GLA_QK] - bc_f)
    ke_b = kf * jnp.exp(bc_b - gk[:, GLA_QK:])
    qd_ref[rows, :GLA_QK] = qd_f.astype(qd_ref.dtype)
    qd_ref[rows, GLA_QK:] = qd_b.astype(qd_ref.dtype)

    lane = lax.broadcasted_iota(jnp.int32, (tile, GLA_QK), 1)

    def stack_heads(qd):
        return jnp.concatenate(
            [jnp.where((lane >= h * GLA_KEY_DIM) & (lane < (h + 1) * GLA_KEY_DIM), qd, 0.0)
             for h in range(GLA_HEADS)], axis=0).astype(BF16)

    att_f = _dot_nt(stack_heads(qd_f), ki_f)
    att_b = _dot_nt(stack_heads(qd_b), ki_b)
    yield
    row = lax.broadcasted_iota(jnp.int32, (tile, tile), 0)
    col = lax.broadcasted_iota(jnp.int32, (tile, tile), 1)
    same = (row & -t) == (col & -t)
    m_f = same & (row >= col)
    m_b = same & (row <= col)
    for h in range(GLA_HEADS):
        hrows = slice(h * tile, (h + 1) * tile)
        att = (jnp.where(m_f, att_f[hrows, :], 0.0) + jnp.where(m_b, att_b[hrows, :], 0.0)).astype(BF16)
        oi_ref[rows, h * GLA_VAL_DIM:(h + 1) * GLA_VAL_DIM] = _dot(
            att, v[:, h * GLA_VAL_DIM:(h + 1) * GLA_VAL_DIM]).astype(oi_ref.dtype)
    yield
    ket_f = ke_f.T.astype(BF16)
    ket_b = ke_b.T.astype(BF16)
    cmask = cmask_ref[...]
    st = []
    for h in range(GLA_HEADS):
        vh = v[:, h * GLA_VAL_DIM:(h + 1) * GLA_VAL_DIM]
        v_exp = jnp.concatenate([vh] * nch, axis=1) * cmask
        lhs = jnp.concatenate([ket_f[h * GLA_KEY_DIM:(h + 1) * GLA_KEY_DIM, :],
                               ket_b[h * GLA_KEY_DIM:(h + 1) * GLA_KEY_DIM, :]], axis=0)
        st.append(_dot(lhs, v_exp))
    tots = jnp.concatenate([r[:, :GLA_QK] for r in last] + [r[:, GLA_QK:] for r in last], axis=0)
    tots_ref[blk] = tots
    results.append((blk, _gla_ecol(tots), st))
    yield


def _gla_local_stages(q_ref, k_ref, v_ref, glr_ref, w2_ref, gb_ref, tril_ref, cmask_ref,
                      oi_ref, qd_ref, sf_ref, tots_ref, sb_ref, s_ref):
    i = pl.program_id(1)
    nch = GLA_TILE_CHUNKS
    n_blk = q_ref.shape[0] // GLA_TILE

    @pl.when(i == 0)
    def _():
        s_ref[...] = jnp.zeros_like(s_ref)

    results = []
    live = [_gla_local_block(blk, q_ref, k_ref, v_ref, glr_ref, w2_ref, gb_ref, tril_ref, cmask_ref, oi_ref,
                             qd_ref, tots_ref, results) for blk in range(n_blk)]
    while live:
        live = [g for g in live if next(g, StopIteration) is not StopIteration]
        yield

    for blk, ecol, st in sorted(results, key=lambda r: -r[0]):
        def local_state(c, lo):
            return jnp.concatenate([st[h][lo:lo + GLA_KEY_DIM, c * GLA_VAL_DIM:(c + 1) * GLA_VAL_DIM]
                                    for h in range(GLA_HEADS)], axis=1)

        for c in reversed(range(nch)):
            sf_ref[blk * nch + c] = local_state(c, 0).astype(sf_ref.dtype)
            sb_ref[blk * nch + c] = s_ref[...].astype(sb_ref.dtype)
            s_ref[...] = s_ref[...] * _gla_decay_matrix(ecol, nch + c) + local_state(c, GLA_KEY_DIM)
        yield


def _gla_final_stages(oi_ref, qd_ref, sf_ref, tots_ref, sb_ref, r_ref, ng_ref, o_ref, s_ref):
    i = pl.program_id(1)
    t = GLA_CHUNK
    nch = GLA_TILE_CHUNKS
    n_blk = oi_ref.shape[0] // GLA_TILE

    @pl.when(i == 0)
    def _():
        s_ref[...] = jnp.zeros_like(s_ref)

    ng = ng_ref[...]
    ecol = [_gla_ecol(tots_ref[blk]) for blk in range(n_blk)]
    chunks = range(n_blk * nch)
    rows = [pl.ds(c * t, t) for c in chunks]
    s_f = []
    for c in chunks:
        s_f.append(s_ref[...].astype(BF16))
        s_ref[...] = s_ref[...] * _gla_decay_matrix(ecol[c // nch], c % nch) + sf_ref[c].astype(F32)
    yield
    inter_f = [_dot(qd_ref[rows[c], :GLA_QK], _gla_block_diag(s_f[c])) for c in chunks]
    yield
    inter_b = [_dot(qd_ref[rows[c], GLA_QK:], _gla_block_diag(sb_ref[c])) for c in chunks]
    yield
    for blk in range(n_blk):
        brows = pl.ds(blk * GLA_TILE, GLA_TILE)
        o = jnp.concatenate([oi_ref[rows[c], :].astype(F32) + inter_f[c] + inter_b[c]
                             for c in range(blk * nch, (blk + 1) * nch)], axis=0)
        rf = r_ref[brows, :].astype(F32)
        o = jnp.concatenate([_rms(o[:, h * GLA_VAL_DIM:(h + 1) * GLA_VAL_DIM], ng[:, h * GLA_VAL_DIM:(h + 1) * GLA_VAL_DIM])
                             for h in range(GLA_HEADS)], axis=1)
        o_ref[brows, :] = (o * (rf * _sigmoid(rf))).astype(o_ref.dtype)
        yield


def _gla_constants():
    r = np.arange(GLA_TILE)
    tril = ((r[:, None] >= r[None, :]) & (r[:, None] // GLA_CHUNK == r[None, :] // GLA_CHUNK)).astype(np.float32)
    ccol = np.arange(GLA_TILE_CHUNKS * GLA_VAL_DIM) // GLA_VAL_DIM
    cmask = (r[:, None] // GLA_CHUNK == ccol[None, :]).astype(np.float32)
    return jnp.asarray(tril, BF16), jnp.asarray(cmask, BF16)


def _local_kernel(xbc_ref, xprev_ref, xnext_ref, dt_ref, q_ref, k_ref, v_ref, glr_ref,
                  cw_ref, cb_ref, alog_ref, dtb_ref, dexp_ref, w2_ref, gb_ref,
                  stril_ref, striu_ref, ef_ref, eb_ref, shift_ref, gtril_ref, cmask_ref,
                  yp_ref, cs_ref, e_ref, ssf_ref, dec_ref, ssb_ref, oi_ref, qd_ref, gsf_ref, tots_ref, gsb_ref,
                  xe_ref, s_ssd_ref, s_gla_ref):
    _run_staged([
        _ssd_local_stages(xbc_ref, xprev_ref, xnext_ref, dt_ref, cw_ref, cb_ref, alog_ref, dtb_ref, dexp_ref,
                          stril_ref, striu_ref, ef_ref, eb_ref, shift_ref,
                          yp_ref, cs_ref, e_ref, ssf_ref, dec_ref, ssb_ref, xe_ref, s_ssd_ref),
        _gla_local_stages(q_ref, k_ref, v_ref, glr_ref, w2_ref, gb_ref, gtril_ref, cmask_ref,
                          oi_ref, qd_ref, gsf_ref, tots_ref, gsb_ref, s_gla_ref),
    ])


def _final_kernel(yp_ref, cs_ref, e_ref, z_ref, ssf_ref, dec_ref, ssb_ref, oi_ref, qd_ref, gsf_ref, tots_ref, gsb_ref,
                  r_ref, pin_ref, pprev_ref, pnext_ref, sng_ref, gng_ref, pw_ref, pscale_ref, ef_ref, eb_ref, band_ref,
                  ssd_out_ref, gla_out_ref, pout_ref, pe_ref, s_ssd_ref, s_gla_ref, *, seq_len):
    _run_staged([
        _ssd_final_stages(yp_ref, cs_ref, e_ref, z_ref, ssf_ref, dec_ref, ssb_ref, sng_ref, ef_ref, eb_ref,
                          ssd_out_ref, s_ssd_ref),
        _gla_final_stages(oi_ref, qd_ref, gsf_ref, tots_ref, gsb_ref, r_ref, gng_ref, gla_out_ref, s_gla_ref),
        _pool_stages(pin_ref, pprev_ref, pnext_ref, band_ref, pw_ref, pscale_ref, pout_ref, pe_ref, seq_len),
    ])


def _mixers(proj, p, layer, b, l, tl):
    seq = lambda name: proj[name].reshape(b, l, -1)
    n_tiles = l // tl
    nci = tl // SSD_CHUNK
    n_blk = tl // GLA_TILE
    nch = n_blk * GLA_TILE_CHUNKS
    stril, striu, ef, eb, shift = _ssd_constants()
    gtril, cmask = _gla_constants()
    band = jnp.asarray(np.stack([_band_matrix(range(-(w // 2), w - w // 2)) for w in POOL_WINDOWS]), BF16)
    xprev, xnext = _halo_specs(tl, SSD_CONV_CH, n_tiles, reverse=True)
    pprev, pnext = _halo_specs(tl, BRANCH_WIDTH, n_tiles)
    rev = lambda width: _tile_spec(tl, width, n_tiles)
    fwd = lambda width: _tile_spec(tl, width)
    act = lambda width: jax.ShapeDtypeStruct((b, l, width), BF16)
    ssd_state = jax.ShapeDtypeStruct((b, l // SSD_CHUNK, SSD_STATE, BRANCH_WIDTH), BF16)
    gla_state = jax.ShapeDtypeStruct((b, l // GLA_CHUNK, GLA_KEY_DIM, BRANCH_WIDTH), BF16)
    xbc, pin = seq("xbc"), seq("pool")
    yp, cs, e, ssf, dec, ssb, oi, qd, gsf, tots, gsb = pl.pallas_call(
        _local_kernel,
        grid=(b, n_tiles),
        in_specs=[rev(SSD_CONV_CH), xprev, xnext, rev(LANE), rev(GLA_QK), rev(GLA_QK), rev(BRANCH_WIDTH), rev(LANE),
                  _wspec(p["ssd_conv_w"], layer), _wspec(p["ssd_conv_b"], layer), _wspec(p["ssd_a_log"], layer),
                  _wspec(p["ssd_dt_bias"], layer), _wspec(p["ssd_d"], layer), _wspec(p["gla_w2"], layer),
                  _wspec(p["gla_gate_b"], layer),
                  _cspec(stril), _cspec(striu), _cspec(ef), _cspec(eb), _cspec(shift), _cspec(gtril), _cspec(cmask)],
        out_specs=[rev(BRANCH_WIDTH), rev(LANE), rev(LANE), _chunk_spec(nci, SSD_STATE, BRANCH_WIDTH, n_tiles),
                   _chunk_spec(nci, 8, BRANCH_WIDTH, n_tiles), _chunk_spec(nci, SSD_STATE, BRANCH_WIDTH, n_tiles),
                   rev(BRANCH_WIDTH), rev(2 * GLA_QK), _chunk_spec(nch, GLA_KEY_DIM, BRANCH_WIDTH, n_tiles),
                   _chunk_spec(n_blk, 8, GLA_QK, n_tiles), _chunk_spec(nch, GLA_KEY_DIM, BRANCH_WIDTH, n_tiles)],
        out_shape=[act(BRANCH_WIDTH), act(LANE), act(LANE), ssd_state,
                   jax.ShapeDtypeStruct((b, l // SSD_CHUNK, 8, BRANCH_WIDTH), F32), ssd_state,
                   act(BRANCH_WIDTH), act(2 * GLA_QK), gla_state,
                   jax.ShapeDtypeStruct((b, l // GLA_TILE, 8, GLA_QK), F32), gla_state],
        scratch_shapes=[pltpu.VMEM((tl + 2 * HALO, SSD_CONV_CH), BF16),
                        pltpu.VMEM((SSD_STATE, BRANCH_WIDTH), F32), pltpu.VMEM((GLA_KEY_DIM, BRANCH_WIDTH), F32)],
        compiler_params=_params(2),
        name="mix_local",
    )(xbc, xbc, xbc, seq("dtg"), seq("q"), seq("k"), seq("gv"), seq("dtg"),
      p["ssd_conv_w"], p["ssd_conv_b"], p["ssd_a_log"], p["ssd_dt_bias"], p["ssd_d"], p["gla_w2"], p["gla_gate_b"],
      stril, striu, ef, eb, shift, gtril, cmask)

    ssd_out, gla_out, pool_out = pl.pallas_call(
        functools.partial(_final_kernel, seq_len=l),
        grid=(b, n_tiles),
        in_specs=[fwd(BRANCH_WIDTH), fwd(LANE), fwd(LANE), fwd(BRANCH_WIDTH), _chunk_spec(nci, SSD_STATE, BRANCH_WIDTH),
                  _chunk_spec(nci, 8, BRANCH_WIDTH), _chunk_spec(nci, SSD_STATE, BRANCH_WIDTH),
                  fwd(BRANCH_WIDTH), fwd(2 * GLA_QK), _chunk_spec(nch, GLA_KEY_DIM, BRANCH_WIDTH),
                  _chunk_spec(n_blk, 8, GLA_QK), _chunk_spec(nch, GLA_KEY_DIM, BRANCH_WIDTH), fwd(BRANCH_WIDTH),
                  fwd(BRANCH_WIDTH), pprev, pnext,
                  _wspec(p["ssd_norm"], layer), _wspec(p["gla_norm"], layer), _wspec(p["pool_w"], layer),
                  _wspec(p["pool_scale"], layer), _cspec(ef), _cspec(eb), _cspec(band)],
        out_specs=[fwd(BRANCH_WIDTH), fwd(BRANCH_WIDTH), fwd(BRANCH_WIDTH)],
        out_shape=[act(BRANCH_WIDTH), act(BRANCH_WIDTH), act(BRANCH_WIDTH)],
        scratch_shapes=[pltpu.VMEM((tl + 2 * HALO, BRANCH_WIDTH), BF16), pltpu.VMEM((SSD_STATE, BRANCH_WIDTH), F32),
                        pltpu.VMEM((GLA_KEY_DIM, BRANCH_WIDTH), F32)],
        compiler_params=_params(2),
        name="mix_final",
    )(yp, cs, e, seq("z"), ssf, dec, ssb, oi, qd, gsf, tots, gsb, seq("r"), pin, pin, pin,
      p["ssd_norm"], p["gla_norm"], p["pool_w"], p["pool_scale"], ef, eb, band)
    return ssd_out, pool_out, gla_out


def _sgu_stages(u_ref, v_ref, ng_ref, w_ref, bias_ref, out_ref, slot):
    gd = BRANCH_WIDTH // SGU_GROUPS
    for r in range(0, u_ref.shape[0], SGU_CHUNK):
        vf = _rms(_gelu_tanh(v_ref[r:r + SGU_CHUNK, :].astype(F32)), ng_ref[...]).astype(BF16)
        yield
        mixed = [_dot(w_ref[g], vf[:, g * gd:(g + 1) * gd]) for g in range(SGU_GROUPS)]
        uf = _gelu_tanh(u_ref[r:r + SGU_CHUNK, :].astype(F32))
        yield
        out_ref[slot, r:r + SGU_CHUNK, :] = (uf * (jnp.concatenate(mixed, axis=1) + bias_ref[...])).astype(BF16)
        yield


def _merge_ffn_stages(rows, x_ref, branches, gate_ref, wb_ref, wo_ref, gpost_ref, gpre_ref, w1_ref, w2_ref, gffn_ref,
                      o_ref):
    merged = None
    for n, load in branches:
        term = gate_ref[rows, n * D_MODEL:(n + 1) * D_MODEL].astype(F32) * _dot(load(rows), wb_ref[n])
        merged = term if merged is None else merged + term
        yield
    x1 = x_ref[rows, :] + _rms(_dot(merged.astype(BF16), wo_ref[...]), gpost_ref[...])
    h2 = _rms(x1, gpre_ref[...]).astype(BF16)
    yield
    ff = None
    for c in range(0, D_FF, D_MODEL):
        hid = jnp.square(jnp.maximum(_dot(h2, w1_ref[:, c:c + D_MODEL]), 0.0)).astype(BF16)
        yield
        part = _dot(hid, w2_ref[c:c + D_MODEL, :])
        ff = part if ff is None else ff + part
        yield
    o_ref[rows, :] = x1 + _rms(ff, gffn_ref[...])


def _merge_ffn_kernel(x_ref, b0_ref, b1_ref, u0_ref, v0_ref, un_ref, vn_ref, b3_ref, gate_ref, sng_ref, sw_ref,
                      sbias_ref, wb_ref, wo_ref, gpost_ref, gpre_ref, w1_ref, w2_ref, gffn_ref, o_ref, sgu_ref):
    i = pl.program_id(0)

    @pl.when(i == 0)
    def _():
        _run_staged([_sgu_stages(u0_ref, v0_ref, sng_ref, sw_ref, sbias_ref, sgu_ref, 0)])

    branches = ((0, lambda rows: b0_ref[rows, :]), (1, lambda rows: b1_ref[rows, :]),
                (3, lambda rows: b3_ref[rows, :]), (2, lambda rows: sgu_ref[i % 2, rows, :]))
    half = x_ref.shape[0] // 2
    _run_staged([
        _merge_ffn_stages(pl.ds(0, half), x_ref, branches, gate_ref, wb_ref, wo_ref, gpost_ref, gpre_ref, w1_ref,
                          w2_ref, gffn_ref, o_ref),
        _merge_ffn_stages(pl.ds(half, half), x_ref, branches, gate_ref, wb_ref, wo_ref, gpost_ref, gpre_ref, w1_ref,
                          w2_ref, gffn_ref, o_ref),
        _sgu_stages(un_ref, vn_ref, sng_ref, sw_ref, sbias_ref, sgu_ref, (i + 1) % 2)])


def _merge_ffn(x2d, ssd_out, pool_out, sgu_u, sgu_v, gla_out, gate, p, layer, tm):
    n = x2d.shape[0]
    steps = n // tm
    row = lambda w: pl.BlockSpec((tm, w), lambda i: (i, 0))
    first = pl.BlockSpec((tm, BRANCH_WIDTH), lambda i: (0, 0), pipeline_mode=pl.Buffered(1))
    ahead = pl.BlockSpec((tm, BRANCH_WIDTH), lambda i: (jnp.minimum(i + 1, steps - 1), 0))
    return pl.pallas_call(
        _merge_ffn_kernel,
        grid=(steps,),
        in_specs=[row(D_MODEL), row(BRANCH_WIDTH), row(BRANCH_WIDTH), first, first, ahead, ahead, row(BRANCH_WIDTH),
                  row(N_BRANCH * D_MODEL),
                  _wspec(p["sgu_norm"], layer), _wspec(p["sgu_w"], layer), _wspec(p["sgu_bias"], layer),
                  _wspec(p["w_branch"], layer), _wspec(p["w_out"], layer), _wspec(p["norm_mix_post"], layer),
                  _wspec(p["norm_ffn_pre"], layer), _wspec(p["w_ff1"], layer), _wspec(p["w_ff2"], layer),
                  _wspec(p["norm_ffn_post"], layer)],
        out_specs=row(D_MODEL),
        out_shape=jax.ShapeDtypeStruct((n, D_MODEL), F32),
        scratch_shapes=[pltpu.VMEM((2, tm, BRANCH_WIDTH), BF16)],
        compiler_params=_params(1),
        name="merge_ffn",
    )(x2d, ssd_out, pool_out, sgu_u, sgu_v, sgu_u, sgu_v, gla_out, gate, p["sgu_norm"], p["sgu_w"], p["sgu_bias"],
      p["w_branch"], p["w_out"], p["norm_mix_post"], p["norm_ffn_pre"], p["w_ff1"], p["w_ff2"],
      p["norm_ffn_post"])


def _pad_lanes(a, width):
    return jnp.pad(a, [(0, 0)] * (a.ndim - 1) + [(0, width - a.shape[-1])])


def _prepare_params(norm_mix_pre, w_in, ssd_conv_w, ssd_conv_b, ssd_a_log, ssd_dt_bias, ssd_d, ssd_norm, pool_w,
                    pool_scale, sgu_norm, sgu_w, sgu_b, gla_gate_w2, gla_gate_b, gla_norm, w_branch, w_out,
                    norm_mix_post, norm_ffn_pre, w_ff1, w_ff2, norm_ffn_post):
    depth = w_in.shape[0]
    pieces = []
    for _, sources, padded, _ in PROJ_GROUPS:
        cols = [w_in[:, :, _REF_COLS[s][0]:_REF_COLS[s][0] + _REF_COLS[s][1]] for s in sources]
        pieces.append(_pad_lanes(jnp.concatenate(cols, axis=-1), padded))
    row = lambda a: a.reshape(depth, 1, -1).astype(F32)
    lo = SSD_DT_LANES
    w2 = jnp.zeros((depth, LANE, 2 * GLA_QK), F32)
    w2 = w2.at[:, lo:lo + GLA_GATE_RANK, :GLA_QK].set(gla_gate_w2[:, 0])
    w2 = w2.at[:, lo + GLA_GATE_RANK:lo + 2 * GLA_GATE_RANK, GLA_QK:].set(gla_gate_w2[:, 1])
    return {
        "norm_mix_pre": row(norm_mix_pre),
        "w_in": jnp.concatenate(pieces, axis=-1).astype(BF16),
        "ssd_conv_w": jnp.pad(ssd_conv_w.astype(F32), [(0, 0), (0, 8 - SSD_CONV), (0, 0)]),
        "ssd_conv_b": row(ssd_conv_b),
        "ssd_a_log": _pad_lanes(row(ssd_a_log), LANE),
        "ssd_dt_bias": _pad_lanes(row(ssd_dt_bias), LANE),
        "ssd_d": row(jnp.repeat(ssd_d, SSD_HEAD_DIM, axis=-1)),
        "ssd_norm": row(ssd_norm),
        "pool_w": pool_w.astype(BF16),
        "pool_scale": row(pool_scale),
        "sgu_norm": row(sgu_norm),
        "sgu_w": sgu_w.astype(BF16),
        "sgu_bias": jnp.repeat(jnp.swapaxes(sgu_b, 1, 2), BRANCH_WIDTH // SGU_GROUPS, axis=-1).astype(F32),
        "gla_w2": w2.astype(BF16),
        "gla_gate_b": row(gla_gate_b),
        "gla_norm": row(gla_norm),
        "w_branch": w_branch.astype(BF16),
        "w_out": w_out.astype(BF16),
        "norm_mix_post": row(norm_mix_post),
        "norm_ffn_pre": row(norm_ffn_pre),
        "w_ff1": w_ff1.astype(BF16),
        "w_ff2": w_ff2.astype(BF16),
        "norm_ffn_post": row(norm_ffn_post),
    }


def _tiles(seq_len):
    return min(512, seq_len), min(512, seq_len), min(1024, seq_len)


def _trunk(x, p):
    b, l, d = x.shape
    tm_proj, tm_ffn, tl = _tiles(l)
    x2d = x.reshape(b * l, d)
    for layer in range(DEPTH):
        proj = dict(zip([g[0] for g in PROJ_GROUPS], _in_proj(x2d, p["norm_mix_pre"], p["w_in"], layer, tm_proj)))
        flat = lambda a: a.reshape(b * l, BRANCH_WIDTH)
        ssd_out, pool_out, gla_out = _mixers(proj, p, layer, b, l, tl)
        x2d = _merge_ffn(x2d, flat(ssd_out), flat(pool_out), proj["u"], proj["v"], flat(gla_out), proj["gate"],
                         p, layer, tm_ffn)
    return x2d.reshape(b, l, d)


def kernel(x_prompt, x_sample, norm_mix_pre, w_in, ssd_conv_w, ssd_conv_b, ssd_a_log, ssd_dt_bias, ssd_d, ssd_norm, pool_w, pool_scale, sgu_norm, sgu_w, sgu_b, gla_gate_w2, gla_gate_b, gla_norm, w_branch, w_out, norm_mix_post, norm_ffn_pre, w_ff1, w_ff2, norm_ffn_post):
    p = _prepare_params(norm_mix_pre, w_in, ssd_conv_w, ssd_conv_b, ssd_a_log, ssd_dt_bias, ssd_d, ssd_norm, pool_w,
                        pool_scale, sgu_norm, sgu_w, sgu_b, gla_gate_w2, gla_gate_b, gla_norm, w_branch, w_out,
                        norm_mix_post, norm_ffn_pre, w_ff1, w_ff2, norm_ffn_post)
    return (_trunk(x_prompt, p), _trunk(x_sample, p))
```

```python
import functools

import jax
import jax.numpy as jnp
import numpy as np
from jax import lax
from jax.experimental import pallas as pl
from jax.experimental.pallas import tpu as pltpu

F32 = jnp.float32
BF16 = jnp.bfloat16

D_MODEL = 1024
DEPTH = 4
BRANCH_WIDTH = 512
N_BRANCH = 4
SSD_HEADS = 8
SSD_HEAD_DIM = 64
SSD_GROUPS = 2
SSD_STATE = 64
SSD_CONV = 5
SSD_CHUNK = 128
SSD_CONV_CH = 768
POOL_WINDOWS = (2, 4, 8, 16)
POOL_GROUP_DIM = 128
POOL_CHUNK = 128
SGU_CHUNK = 128
SGU_GROUPS = 4
GLA_HEADS = 4
GLA_KEY_DIM = 64
GLA_VAL_DIM = 128
GLA_GATE_RANK = 16
GLA_GATE_NORMALIZER = 16.0
GLA_CHUNK = 64
GLA_QK = GLA_HEADS * GLA_KEY_DIM
GLA_TILE = 256
GLA_TILE_CHUNKS = GLA_TILE // GLA_CHUNK
D_FF = 4096
RMS_EPS = 1e-6

LANE = 128
HALO = 16
VMEM_LIMIT = 56 * 1024 * 1024

_REF_COLS = {}
_off = 0
for _name, _width in (("z", 512), ("xbc", 768), ("dt", 16), ("pool", 512), ("u", 512), ("v", 512), ("q", 256),
                      ("k", 256), ("gv", 512), ("r", 512), ("glr", 32), ("gate", 4096)):
    _REF_COLS[_name] = (_off, _width)
    _off += _width
SSD_DT_LANES = 2 * SSD_HEADS
PROJ_GROUPS = (
    ("gate", ("gate",), 4096, BF16), ("z", ("z",), 512, BF16), ("xbc", ("xbc",), 768, BF16),
    ("dtg", ("dt", "glr"), LANE, F32), ("pool", ("pool",), 512, BF16), ("u", ("u",), 512, BF16),
    ("v", ("v",), 512, BF16), ("q", ("q",), 256, BF16), ("k", ("k",), 256, BF16), ("gv", ("gv",), 512, BF16),
    ("r", ("r",), 512, BF16),
)
PROJ_CHUNK = 512


LOG2_E = float(np.log2(np.e))


def _sigmoid(x):
    return 1.0 / (1.0 + jnp.exp2(x * (-LOG2_E)))


def _softplus(x):
    return jnp.maximum(x, 0.0) + jnp.log(1.0 + jnp.exp(-jnp.abs(x)))


def _gelu_tanh(x):
    c = -2.0 * np.sqrt(2.0 / np.pi) * LOG2_E
    return x / (1.0 + jnp.exp2(x * (c + (c * 0.044715) * (x * x))))


def _rms(x, g):
    return x * lax.rsqrt(jnp.mean(x * x, axis=-1, keepdims=True) + RMS_EPS) * g


def _dot(a, b):
    return jnp.dot(a, b, preferred_element_type=F32)


def _dot_tn(a, b):
    return lax.dot_general(a, b, (((0,), (0,)), ((), ())), preferred_element_type=F32)


def _dot_nt(a, b):
    return lax.dot_general(a, b, (((1,), (1,)), ((), ())), preferred_element_type=F32)


def _split_bf16(x):
    hi = x.astype(BF16)
    lo = (x - hi.astype(F32)).astype(BF16)
    return hi, lo


def _dot_split_rhs(a_bf16, x):
    hi, lo = _split_bf16(x)
    return _dot(a_bf16, hi) + _dot(a_bf16, lo)


def _dot_split_lhs(x, b_bf16):
    hi, lo = _split_bf16(x)
    return _dot(hi, b_bf16) + _dot(lo, b_bf16)


def _wspec(arr, layer):
    nd = arr.ndim - 1
    return pl.BlockSpec((None,) + tuple(arr.shape[1:]), lambda *_: (layer,) + (0,) * nd,
                        pipeline_mode=pl.Buffered(1))


def _cspec(arr):
    nd = arr.ndim
    return pl.BlockSpec(tuple(arr.shape), lambda *_: (0,) * nd, pipeline_mode=pl.Buffered(1))


def _params(n_grid):
    return pltpu.CompilerParams(dimension_semantics=("arbitrary",) * n_grid, vmem_limit_bytes=VMEM_LIMIT)


def _run_staged(generators):
    live = list(generators)
    while live:
        live = [g for g in live if next(g, StopIteration) is not StopIteration]


def _in_proj_kernel(x_ref, g_ref, w_ref, *out_refs):
    h = _rms(x_ref[...], g_ref[...]).astype(BF16)
    off = 0
    for o_ref, (name, _, width, dtype) in zip(out_refs, PROJ_GROUPS):
        for c in range(0, width, PROJ_CHUNK):
            cw = min(PROJ_CHUNK, width - c)
            acc = _dot(h, w_ref[:, off + c:off + c + cw])
            if name == "gate":
                acc = _sigmoid(acc)
            o_ref[:, c:c + cw] = acc.astype(dtype)
        off += width


def _in_proj(x2d, norm_g, w_in_p, layer, tm):
    n = x2d.shape[0]
    out_shape = [jax.ShapeDtypeStruct((n, g[2]), g[3]) for g in PROJ_GROUPS]
    out_specs = [pl.BlockSpec((tm, g[2]), lambda i: (i, 0)) for g in PROJ_GROUPS]
    return pl.pallas_call(
        _in_proj_kernel,
        grid=(n // tm,),
        in_specs=[pl.BlockSpec((tm, D_MODEL), lambda i: (i, 0)), _wspec(norm_g, layer), _wspec(w_in_p, layer)],
        out_specs=out_specs,
        out_shape=out_shape,
        compiler_params=_params(1),
        name="in_proj",
    )(x2d, norm_g, w_in_p)


def _tile_spec(tl, width, n_tiles=None):
    if n_tiles is None:
        return pl.BlockSpec((None, tl, width), lambda b, i: (b, i, 0))
    return pl.BlockSpec((None, tl, width), lambda b, i: (b, n_tiles - 1 - i, 0))


def _chunk_spec(per_tile, rows, width, n_tiles=None):
    if n_tiles is None:
        return pl.BlockSpec((None, per_tile, rows, width), lambda b, i: (b, i, 0, 0))
    return pl.BlockSpec((None, per_tile, rows, width), lambda b, i: (b, n_tiles - 1 - i, 0, 0))


def _halo_specs(tl, width, n_tiles, reverse=False):
    per = tl // HALO
    last = n_tiles * per - 1

    def tile(i):
        return n_tiles - 1 - i if reverse else i

    prev = pl.BlockSpec((None, HALO, width), lambda b, i: (b, jnp.maximum(tile(i) * per - 1, 0), 0))
    nxt = pl.BlockSpec((None, HALO, width), lambda b, i: (b, jnp.minimum((tile(i) + 1) * per, last), 0))
    return prev, nxt


def _fill_halo_buffer(ext_ref, cur_ref, prev_ref, next_ref, is_first, is_last):
    tl = cur_ref.shape[0]
    zeros = jnp.zeros((HALO, ext_ref.shape[1]), ext_ref.dtype)
    ext_ref[0:HALO, :] = jnp.where(is_first, zeros, prev_ref[...])
    ext_ref[HALO:HALO + tl, :] = cur_ref[...]
    ext_ref[HALO + tl:2 * HALO + tl, :] = jnp.where(is_last, zeros, next_ref[...])


HALO_WINDOW = 128 + 2 * HALO


def _band_matrix(offsets):
    r = np.arange(128)[:, None]
    w = np.arange(HALO_WINDOW)[None, :]
    return sum((w == r + HALO + off).astype(np.float32) for off in offsets)


def _ssd_conv(xe_ref, r0, shift_ref, cw_ref, cb_ref):
    t = SSD_CHUNK
    pad = SSD_CONV // 2
    win = xe_ref[pl.ds(r0, HALO_WINDOW), :]
    shifted = _dot(shift_ref[...], win)
    acc = cb_ref[...] + cw_ref[pad:pad + 1, :] * win[HALO:HALO + t, :].astype(F32)
    for n, j in enumerate(j for j in range(SSD_CONV) if j != pad):
        acc = acc + cw_ref[j:j + 1, :] * shifted[n * t:(n + 1) * t, :]
    return acc * _sigmoid(acc)


def _ssd_decay_terms(dt_raw, dtb_ref, alog_ref, tril_ref, triu_ref):
    t = dt_raw.shape[0]
    lane = lax.broadcasted_iota(jnp.int32, (t, LANE), 1)
    used = lane < 2 * SSD_HEADS
    dt = jnp.where(used, _softplus(dt_raw + dtb_ref[...]), 0.0)
    a = jnp.where(used[0:1], -jnp.exp(alog_ref[...]), 0.0)
    dta = dt * a
    cf = _dot_split_rhs(tril_ref[...], dta)
    cr = _dot_split_rhs(triu_ref[...], dta)
    fwd = lane < SSD_HEADS
    acs = jnp.where(fwd, cf, cr)
    tot = jnp.where(fwd[0:1], cf[t - 1:t, :], cr[0:1, :])
    return dt, acs, tot


def _ssd_compact(full):
    half = BRANCH_WIDTH // SSD_GROUPS
    return jnp.concatenate([full[:SSD_STATE, :half], full[SSD_STATE:, half:]], axis=1)


def _ssd_expand(compact):
    half = BRANCH_WIDTH // SSD_GROUPS
    z = jnp.zeros((SSD_STATE, half), compact.dtype)
    return jnp.concatenate([jnp.concatenate([compact[:, :half], z], axis=1),
                            jnp.concatenate([z, compact[:, half:]], axis=1)], axis=0)


def _ssd_local_stages(xbc_ref, xprev_ref, xnext_ref, dt_ref, cw_ref, cb_ref, alog_ref, dtb_ref, dexp_ref,
                      tril_ref, triu_ref, ef_ref, eb_ref, shift_ref,
                      yp_ref, cs_ref, e_ref, sf_ref, dec_ref, sb_ref, xe_ref, s_ref):
    i = pl.program_id(1)
    n = pl.num_programs(1)
    nci = xbc_ref.shape[0] // SSD_CHUNK
    t = SSD_CHUNK

    @pl.when(i == 0)
    def _():
        s_ref[...] = jnp.zeros_like(s_ref)

    _fill_halo_buffer(xe_ref, xbc_ref, xprev_ref, xnext_ref, i == n - 1, i == 0)

    lane = lax.broadcasted_iota(jnp.int32, (t, LANE), 1)
    low = lane < SSD_STATE
    row = lax.broadcasted_iota(jnp.int32, (t, t), 0)
    col = lax.broadcasted_iota(jnp.int32, (t, t), 1)
    tge = row >= col
    teq = row == col
    hpg = SSD_HEADS // SSD_GROUPS

    chunks = range(nci)
    xc = []
    for c in chunks:
        xc.append(_ssd_conv(xe_ref, c * t, shift_ref, cw_ref, cb_ref))
        yield
    xs = [a[:, :BRANCH_WIDTH] for a in xc]
    bs = [a[:, BRANCH_WIDTH:BRANCH_WIDTH + LANE].astype(BF16) for a in xc]
    cs = [a[:, BRANCH_WIDTH + LANE:] for a in xc]
    terms = [_ssd_decay_terms(dt_ref[pl.ds(c * t, t), :], dtb_ref, alog_ref, tril_ref, triu_ref) for c in chunks]
    dt = [a[0] for a in terms]
    acs = [a[1] for a in terms]
    tot = [a[2] for a in terms]
    yield
    cb = [(_dot_nt(jnp.where(low, cs[c], 0.0).astype(BF16), bs[c]),
           _dot_nt(jnp.where(low, 0.0, cs[c]).astype(BF16), bs[c])) for c in chunks]
    dt_t = [a.T for a in dt]
    src_t = [acs[c].T - jnp.log(dt_t[c]) for c in chunks]
    yield
    w = [(jnp.exp(tot[c] - acs[c]) * dt[c]).astype(BF16) for c in chunks]
    w_f = [_dot(w[c], ef_ref[...]) for c in chunks]
    w_b = [_dot(w[c], eb_ref[...]) for c in chunks]
    dec8 = [jnp.broadcast_to(jnp.exp(tot[c]), (8, LANE)) for c in chunks]
    dec_f = [_dot_split_lhs(dec8[c], ef_ref[...]) for c in chunks]
    dec_b = [_dot_split_lhs(dec8[c], eb_ref[...]) for c in chunks]
    yield

    y_pairs = [[] for _ in chunks]
    for pair in range(SSD_HEADS // 2):
        acc = [None for _ in chunks]
        for sub in range(2):
            h = 2 * pair + sub
            hb = SSD_HEADS + h
            half_mask = low if sub == 0 else jnp.logical_not(low)
            for c in chunks:
                arg = jnp.where(tge, acs[c][:, h:h + 1] - src_t[c][h:h + 1, :],
                                acs[c][:, hb:hb + 1] - src_t[c][hb:hb + 1, :])
                decay = jnp.exp(arg) + jnp.where(teq, dt_t[c][hb:hb + 1, :], 0.0)
                m = (cb[c][h // hpg] * decay).astype(BF16)
                x_half = jnp.where(half_mask, xs[c][:, pair * LANE:(pair + 1) * LANE], 0.0).astype(BF16)
                part = _dot(m, x_half)
                acc[c] = part if acc[c] is None else acc[c] + part
            yield
        for c in chunks:
            y_pairs[c].append(acc[c])

    st_f = [_ssd_compact(_dot_tn(bs[c], (xs[c] * w_f[c]).astype(BF16))) for c in chunks]
    st_b = [_ssd_compact(_dot_tn(bs[c], (xs[c] * w_b[c]).astype(BF16))) for c in chunks]
    yield
    srow = lax.broadcasted_iota(jnp.int32, (8, BRANCH_WIDTH), 0)
    for c in chunks:
        rows = pl.ds(c * t, t)
        y = jnp.concatenate(y_pairs[c], axis=1) + xs[c] * dexp_ref[...]
        yp_ref[rows, :] = y.astype(yp_ref.dtype)
        cs_ref[rows, :] = cs[c].astype(cs_ref.dtype)
        e_ref[rows, :] = jnp.exp(acs[c]).astype(e_ref.dtype)
        dec_ref[c] = jnp.where(srow == 0, dec_f[c], dec_b[c])
        sf_ref[c] = st_f[c].astype(sf_ref.dtype)
        yield
    for c in reversed(chunks):
        sb_ref[c] = s_ref[...].astype(sb_ref.dtype)
        s_ref[...] = s_ref[...] * dec_b[c][0:1, :] + st_b[c]


def _ssd_final_stages(yp_ref, cs_ref, e_ref, z_ref, sf_ref, dec_ref, sb_ref, ng_ref, ef_ref, eb_ref, o_ref, s_ref):
    i = pl.program_id(1)
    nci = yp_ref.shape[0] // SSD_CHUNK
    t = SSD_CHUNK

    @pl.when(i == 0)
    def _():
        s_ref[...] = jnp.zeros_like(s_ref)

    half = BRANCH_WIDTH // SSD_GROUPS
    ng = ng_ref[...]
    chunks = range(nci)
    rows = [pl.ds(c * t, t) for c in chunks]
    s_f = []
    for c in chunks:
        s_f.append(s_ref[...].astype(BF16))
        s_ref[...] = s_ref[...] * dec_ref[c][0:1, :] + sf_ref[c].astype(F32)
    yield
    ex_f = [_dot(e_ref[rows[c], :], ef_ref[...]) for c in chunks]
    ex_b = [_dot(e_ref[rows[c], :], eb_ref[...]) for c in chunks]
    yield
    off_f = [_dot(cs_ref[rows[c], :], _ssd_expand(s_f[c])) for c in chunks]
    off_b = [_dot(cs_ref[rows[c], :], _ssd_expand(sb_ref[c])) for c in chunks]
    yield
    for c in chunks:
        y = yp_ref[rows[c], :].astype(F32) + ex_f[c] * off_f[c] + ex_b[c] * off_b[c]
        zf = z_ref[rows[c], :].astype(F32)
        y = y * (zf * _sigmoid(zf))
        y = jnp.concatenate([_rms(y[:, g * half:(g + 1) * half], ng[:, g * half:(g + 1) * half])
                             for g in range(SSD_GROUPS)], axis=1)
        o_ref[rows[c], :] = y.astype(o_ref.dtype)
        yield


def _ssd_constants():
    t = SSD_CHUNK
    r = np.arange(t)
    tril = (r[:, None] >= r[None, :]).astype(np.float32)
    triu = (r[:, None] <= r[None, :]).astype(np.float32)
    col = np.arange(BRANCH_WIDTH)
    ef = (r[:, None] == (col[None, :] // SSD_HEAD_DIM)).astype(np.float32)
    eb = (r[:, None] == (col[None, :] // SSD_HEAD_DIM) + SSD_HEADS).astype(np.float32)
    pad = SSD_CONV // 2
    shift = np.concatenate([_band_matrix([j - pad]) for j in range(SSD_CONV) if j != pad], axis=0)
    return [jnp.asarray(a, BF16) for a in (tril, triu, ef, eb, shift)]


def _pool_stages(p_ref, pprev_ref, pnext_ref, band_ref, w_ref, scale_ref, o_ref, pe_ref, seq_len):
    i = pl.program_id(1)
    n = pl.num_programs(1)
    tl = p_ref.shape[0]
    t = POOL_CHUNK
    _fill_halo_buffer(pe_ref, p_ref, pprev_ref, pnext_ref, i == 0, i == n - 1)
    tile_start = i * tl
    blocks = [(r0, g) for r0 in range(0, tl, t) for g in range(len(POOL_WINDOWS))]
    cols = lambda g: slice(g * POOL_GROUP_DIM, (g + 1) * POOL_GROUP_DIM)
    sums = []
    for r0, g in blocks:
        sums.append(_dot(band_ref[g], pe_ref[r0:r0 + HALO_WINDOW, cols(g)]))
        if g == len(POOL_WINDOWS) - 1:
            yield
    pooled = []
    for (r0, g), acc in zip(blocks, sums):
        left = POOL_WINDOWS[g] // 2
        right = POOL_WINDOWS[g] - 1 - left
        pos = tile_start + r0 + lax.broadcasted_iota(jnp.int32, (t, 1), 0)
        cnt = (jnp.minimum(pos + right, seq_len - 1) - jnp.maximum(pos - left, 0) + 1).astype(F32)
        pooled.append((acc / cnt - pe_ref[HALO + r0:HALO + r0 + t, cols(g)].astype(F32)).astype(BF16))
        if g == len(POOL_WINDOWS) - 1:
            yield
    mixed = []
    for (r0, g), pb in zip(blocks, pooled):
        mixed.append(_dot(pb, w_ref[g]))
        if g == len(POOL_WINDOWS) - 1:
            yield
    for (r0, g), y in zip(blocks, mixed):
        o_ref[r0:r0 + t, cols(g)] = (y * scale_ref[:, cols(g)]).astype(o_ref.dtype)
    yield


def _gla_decay_matrix(ecol, col):
    return jnp.concatenate(
        [jnp.broadcast_to(ecol[h * GLA_KEY_DIM:(h + 1) * GLA_KEY_DIM, col:col + 1], (GLA_KEY_DIM, GLA_VAL_DIM))
         for h in range(GLA_HEADS)], axis=1)


def _gla_block_diag(compact):
    z = jnp.zeros((GLA_KEY_DIM, GLA_VAL_DIM), compact.dtype)
    return jnp.concatenate(
        [jnp.concatenate([compact[:, h * GLA_VAL_DIM:(h + 1) * GLA_VAL_DIM] if g == h else z
                          for g in range(GLA_HEADS)], axis=1) for h in range(GLA_HEADS)], axis=0)


def _gla_ecol(tots):
    padded = jnp.concatenate([tots, jnp.zeros((LANE - tots.shape[0], GLA_QK), F32)], axis=0)
    return jnp.exp(padded.T)


def _gla_local_block(blk, q_ref, k_ref, v_ref, glr_ref, w2_ref, gb_ref, tril_ref, cmask_ref, oi_ref, qd_ref,
                     tots_ref, results):
    t = GLA_CHUNK
    nch = GLA_TILE_CHUNKS
    tile = GLA_TILE
    rows = pl.ds(blk * tile, tile)

    pre = _dot(glr_ref[rows, :].astype(BF16), w2_ref[...]) + gb_ref[...]
    yield
    gk = -_softplus(-pre) * (1.0 / GLA_GATE_NORMALIZER)
    bc = _dot(tril_ref[...], gk.astype(BF16))
    yield
    last = [bc[c * t + t - 1:c * t + t, :] for c in range(nch)]
    tot = jnp.concatenate([jnp.broadcast_to(row, (t, 2 * GLA_QK)) for row in last], axis=0)
    bc_f = bc[:, :GLA_QK]
    bc_b = bc[:, GLA_QK:]
    rb = tot[:, GLA_QK:] - bc_b + gk[:, GLA_QK:]
    qf = q_ref[rows, :].astype(F32) * (GLA_KEY_DIM ** -0.5)
    kf = k_ref[rows, :].astype(F32)
    v = v_ref[rows, :]
    qd_f = qf * jnp.exp(bc_f)
    qd_b = qf * jnp.exp(rb)
    ki_f = (kf * jnp.exp(-bc_f)).astype(BF16)
    ki_b = (kf * jnp.exp(-rb)).astype(BF16)
    ke_f = kf * jnp.exp(tot[:, :GLA_QK] - bc_f)
    ke_b = kf * jnp.exp(bc_b - gk[:, GLA_QK:])
    qd_ref[rows, :GLA_QK] = qd_f.astype(qd_ref.dtype)
    qd_ref[rows, GLA_QK:] = qd_b.astype(qd_ref.dtype)

    lane = lax.broadcasted_iota(jnp.int32, (tile, GLA_QK), 1)

    def stack_heads(qd):
        return jnp.concatenate(
            [jnp.where((lane >= h * GLA_KEY_DIM) & (lane < (h + 1) * GLA_KEY_DIM), qd, 0.0)
             for h in range(GLA_HEADS)], axis=0).astype(BF16)

    att_f = _dot_nt(stack_heads(qd_f), ki_f)
    att_b = _dot_nt(stack_heads(qd_b), ki_b)
    yield
    row = lax.broadcasted_iota(jnp.int32, (tile, tile), 0)
    col = lax.broadcasted_iota(jnp.int32, (tile, tile), 1)
    same = (row & -t) == (col & -t)
    m_f = same & (row >= col)
    m_b = same & (row <= col)
    for h in range(GLA_HEADS):
        hrows = slice(h * tile, (h + 1) * tile)
        att = (jnp.where(m_f, att_f[hrows, :], 0.0) + jnp.where(m_b, att_b[hrows, :], 0.0)).astype(BF16)
        oi_ref[rows, h * GLA_VAL_DIM:(h + 1) * GLA_VAL_DIM] = _dot(
            att, v[:, h * GLA_VAL_DIM:(h + 1) * GLA_VAL_DIM]).astype(oi_ref.dtype)
    yield
    ket_f = ke_f.T.astype(BF16)
    ket_b = ke_b.T.astype(BF16)
    cmask = cmask_ref[...]
    st = []
    for h in range(GLA_HEADS):
        vh = v[:, h * GLA_VAL_DIM:(h + 1) * GLA_VAL_DIM]
        v_exp = jnp.concatenate([vh] * nch, axis=1) * cmask
        lhs = jnp.concatenate([ket_f[h * GLA_KEY_DIM:(h + 1) * GLA_KEY_DIM, :],
                               ket_b[h * GLA_KEY_DIM:(h + 1) * GLA_KEY_DIM, :]], axis=0)
        st.append(_dot(lhs, v_exp))
    tots = jnp.concatenate([r[:, :GLA_QK] for r in last] + [r[:, GLA_QK:] for r in last], axis=0)
    tots_ref[blk] = tots
    results.append((blk, _gla_ecol(tots), st))
    yield


def _gla_local_stages(q_ref, k_ref, v_ref, glr_ref, w2_ref, gb_ref, tril_ref, cmask_ref,
                      oi_ref, qd_ref, sf_ref, tots_ref, sb_ref, s_ref):
    i = pl.program_id(1)
    nch = GLA_TILE_CHUNKS
    n_blk = q_ref.shape[0] // GLA_TILE

    @pl.when(i == 0)
    def _():
        s_ref[...] = jnp.zeros_like(s_ref)

    results = []
    live = [_gla_local_block(blk, q_ref, k_ref, v_ref, glr_ref, w2_ref, gb_ref, tril_ref, cmask_ref, oi_ref,
                             qd_ref, tots_ref, results) for blk in range(n_blk)]
    while live:
        live = [g for g in live if next(g, StopIteration) is not StopIteration]
        yield

    for blk, ecol, st in sorted(results, key=lambda r: -r[0]):
        def local_state(c, lo):
            return jnp.concatenate([st[h][lo:lo + GLA_KEY_DIM, c * GLA_VAL_DIM:(c + 1) * GLA_VAL_DIM]
                                    for h in range(GLA_HEADS)], axis=1)

        for c in reversed(range(nch)):
            sf_ref[blk * nch + c] = local_state(c, 0).astype(sf_ref.dtype)
            sb_ref[blk * nch + c] = s_ref[...].astype(sb_ref.dtype)
            s_ref[...] = s_ref[...] * _gla_decay_matrix(ecol, nch + c) + local_state(c, GLA_KEY_DIM)
        yield


def _gla_final_stages(oi_ref, qd_ref, sf_ref, tots_ref, sb_ref, r_ref, ng_ref, o_ref, s_ref):
    i = pl.program_id(1)
    t = GLA_CHUNK
    nch = GLA_TILE_CHUNKS
    n_blk = oi_ref.shape[0] // GLA_TILE

    @pl.when(i == 0)
    def _():
        s_ref[...] = jnp.zeros_like(s_ref)

    ng = ng_ref[...]
    ecol = [_gla_ecol(tots_ref[blk]) for blk in range(n_blk)]
    chunks = range(n_blk * nch)
    rows = [pl.ds(c * t, t) for c in chunks]
    s_f = []
    for c in chunks:
        s_f.append(s_ref[...].astype(BF16))
        s_ref[...] = s_ref[...] * _gla_decay_matrix(ecol[c // nch], c % nch) + sf_ref[c].astype(F32)
    yield
    inter_f = [_dot(qd_ref[rows[c], :GLA_QK], _gla_block_diag(s_f[c])) for c in chunks]
    yield
    inter_b = [_dot(qd_ref[rows[c], GLA_QK:], _gla_block_diag(sb_ref[c])) for c in chunks]
    yield
    for blk in range(n_blk):
        brows = pl.ds(blk * GLA_TILE, GLA_TILE)
        o = jnp.concatenate([oi_ref[rows[c], :].astype(F32) + inter_f[c] + inter_b[c]
                             for c in range(blk * nch, (blk + 1) * nch)], axis=0)
        rf = r_ref[brows, :].astype(F32)
        o = jnp.concatenate([_rms(o[:, h * GLA_VAL_DIM:(h + 1) * GLA_VAL_DIM], ng[:, h * GLA_VAL_DIM:(h + 1) * GLA_VAL_DIM])
                             for h in range(GLA_HEADS)], axis=1)
        o_ref[brows, :] = (o * (rf * _sigmoid(rf))).astype(o_ref.dtype)
        yield


def _gla_constants():
    r = np.arange(GLA_TILE)
    tril = ((r[:, None] >= r[None, :]) & (r[:, None] // GLA_CHUNK == r[None, :] // GLA_CHUNK)).astype(np.float32)
    ccol = np.arange(GLA_TILE_CHUNKS * GLA_VAL_DIM) // GLA_VAL_DIM
    cmask = (r[:, None] // GLA_CHUNK == ccol[None, :]).astype(np.float32)
    return jnp.asarray(tril, BF16), jnp.asarray(cmask, BF16)


def _local_kernel(xbc_ref, xprev_ref, xnext_ref, dt_ref, q_ref, k_ref, v_ref, glr_ref,
                  cw_ref, cb_ref, alog_ref, dtb_ref, dexp_ref, w2_ref, gb_ref,
                  stril_ref, striu_ref, ef_ref, eb_ref, shift_ref, gtril_ref, cmask_ref,
                  yp_ref, cs_ref, e_ref, ssf_ref, dec_ref, ssb_ref, oi_ref, qd_ref, gsf_ref, tots_ref, gsb_ref,
                  xe_ref, s_ssd_ref, s_gla_ref):
    _run_staged([
        _ssd_local_stages(xbc_ref, xprev_ref, xnext_ref, dt_ref, cw_ref, cb_ref, alog_ref, dtb_ref, dexp_ref,
                          stril_ref, striu_ref, ef_ref, eb_ref, shift_ref,
                          yp_ref, cs_ref, e_ref, ssf_ref, dec_ref, ssb_ref, xe_ref, s_ssd_ref),
        _gla_local_stages(q_ref, k_ref, v_ref, glr_ref, w2_ref, gb_ref, gtril_ref, cmask_ref,
                          oi_ref, qd_ref, gsf_ref, tots_ref, gsb_ref, s_gla_ref),
    ])


def _final_kernel(yp_ref, cs_ref, e_ref, z_ref, ssf_ref, dec_ref, ssb_ref, oi_ref, qd_ref, gsf_ref, tots_ref, gsb_ref,
                  r_ref, pin_ref, pprev_ref, pnext_ref, sng_ref, gng_ref, pw_ref, pscale_ref, ef_ref, eb_ref, band_ref,
                  ssd_out_ref, gla_out_ref, pout_ref, pe_ref, s_ssd_ref, s_gla_ref, *, seq_len):
    _run_staged([
        _ssd_final_stages(yp_ref, cs_ref, e_ref, z_ref, ssf_ref, dec_ref, ssb_ref, sng_ref, ef_ref, eb_ref,
                          ssd_out_ref, s_ssd_ref),
        _gla_final_stages(oi_ref, qd_ref, gsf_ref, tots_ref, gsb_ref, r_ref, gng_ref, gla_out_ref, s_gla_ref),
        _pool_stages(pin_ref, pprev_ref, pnext_ref, band_ref, pw_ref, pscale_ref, pout_ref, pe_ref, seq_len),
    ])


def _mixers(proj, p, layer, b, l, tl):
    seq = lambda name: proj[name].reshape(b, l, -1)
    n_tiles = l // tl
    nci = tl // SSD_CHUNK
    n_blk = tl // GLA_TILE
    nch = n_blk * GLA_TILE_CHUNKS
    stril, striu, ef, eb, shift = _ssd_constants()
    gtril, cmask = _gla_constants()
    band = jnp.asarray(np.stack([_band_matrix(range(-(w // 2), w - w // 2)) for w in POOL_WINDOWS]), BF16)
    xprev, xnext = _halo_specs(tl, SSD_CONV_CH, n_tiles, reverse=True)
    pprev, pnext = _halo_specs(tl, BRANCH_WIDTH, n_tiles)
    rev = lambda width: _tile_spec(tl, width, n_tiles)
    fwd = lambda width: _tile_spec(tl, width)
    act = lambda width: jax.ShapeDtypeStruct((b, l, width), BF16)
    ssd_state = jax.ShapeDtypeStruct((b, l // SSD_CHUNK, SSD_STATE, BRANCH_WIDTH), BF16)
    gla_state = jax.ShapeDtypeStruct((b, l // GLA_CHUNK, GLA_KEY_DIM, BRANCH_WIDTH), BF16)
    xbc, pin = seq("xbc"), seq("pool")
    yp, cs, e, ssf, dec, ssb, oi, qd, gsf, tots, gsb = pl.pallas_call(
        _local_kernel,
        grid=(b, n_tiles),
        in_specs=[rev(SSD_CONV_CH), xprev, xnext, rev(LANE), rev(GLA_QK), rev(GLA_QK), rev(BRANCH_WIDTH), rev(LANE),
                  _wspec(p["ssd_conv_w"], layer), _wspec(p["ssd_conv_b"], layer), _wspec(p["ssd_a_log"], layer),
                  _wspec(p["ssd_dt_bias"], layer), _wspec(p["ssd_d"], layer), _wspec(p["gla_w2"], layer),
                  _wspec(p["gla_gate_b"], layer),
                  _cspec(stril), _cspec(striu), _cspec(ef), _cspec(eb), _cspec(shift), _cspec(gtril), _cspec(cmask)],
        out_specs=[rev(BRANCH_WIDTH), rev(LANE), rev(LANE), _chunk_spec(nci, SSD_STATE, BRANCH_WIDTH, n_tiles),
                   _chunk_spec(nci, 8, BRANCH_WIDTH, n_tiles), _chunk_spec(nci, SSD_STATE, BRANCH_WIDTH, n_tiles),
                   rev(BRANCH_WIDTH), rev(2 * GLA_QK), _chunk_spec(nch, GLA_KEY_DIM, BRANCH_WIDTH, n_tiles),
                   _chunk_spec(n_blk, 8, GLA_QK, n_tiles), _chunk_spec(nch, GLA_KEY_DIM, BRANCH_WIDTH, n_tiles)],
        out_shape=[act(BRANCH_WIDTH), act(LANE), act(LANE), ssd_state,
                   jax.ShapeDtypeStruct((b, l // SSD_CHUNK, 8, BRANCH_WIDTH), F32), ssd_state,
                   act(BRANCH_WIDTH), act(2 * GLA_QK), gla_state,
                   jax.ShapeDtypeStruct((b, l // GLA_TILE, 8, GLA_QK), F32), gla_state],
        scratch_shapes=[pltpu.VMEM((tl + 2 * HALO, SSD_CONV_CH), BF16),
                        pltpu.VMEM((SSD_STATE, BRANCH_WIDTH), F32), pltpu.VMEM((GLA_KEY_DIM, BRANCH_WIDTH), F32)],
        compiler_params=_params(2),
        name="mix_local",
    )(xbc, xbc, xbc, seq("dtg"), seq("q"), seq("k"), seq("gv"), seq("dtg"),
      p["ssd_conv_w"], p["ssd_conv_b"], p["ssd_a_log"], p["ssd_dt_bias"], p["ssd_d"], p["gla_w2"], p["gla_gate_b"],
      stril, striu, ef, eb, shift, gtril, cmask)

    ssd_out, gla_out, pool_out = pl.pallas_call(
        functools.partial(_final_kernel, seq_len=l),
        grid=(b, n_tiles),
        in_specs=[fwd(BRANCH_WIDTH), fwd(LANE), fwd(LANE), fwd(BRANCH_WIDTH), _chunk_spec(nci, SSD_STATE, BRANCH_WIDTH),
                  _chunk_spec(nci, 8, BRANCH_WIDTH), _chunk_spec(nci, SSD_STATE, BRANCH_WIDTH),
                  fwd(BRANCH_WIDTH), fwd(2 * GLA_QK), _chunk_spec(nch, GLA_KEY_DIM, BRANCH_WIDTH),
                  _chunk_spec(n_blk, 8, GLA_QK), _chunk_spec(nch, GLA_KEY_DIM, BRANCH_WIDTH), fwd(BRANCH_WIDTH),
                  fwd(BRANCH_WIDTH), pprev, pnext,
                  _wspec(p["ssd_norm"], layer), _wspec(p["gla_norm"], layer), _wspec(p["pool_w"], layer),
                  _wspec(p["pool_scale"], layer), _cspec(ef), _cspec(eb), _cspec(band)],
        out_specs=[fwd(BRANCH_WIDTH), fwd(BRANCH_WIDTH), fwd(BRANCH_WIDTH)],
        out_shape=[act(BRANCH_WIDTH), act(BRANCH_WIDTH), act(BRANCH_WIDTH)],
        scratch_shapes=[pltpu.VMEM((tl + 2 * HALO, BRANCH_WIDTH), BF16), pltpu.VMEM((SSD_STATE, BRANCH_WIDTH), F32),
                        pltpu.VMEM((GLA_KEY_DIM, BRANCH_WIDTH), F32)],
        compiler_params=_params(2),
        name="mix_final",
    )(yp, cs, e, seq("z"), ssf, dec, ssb, oi, qd, gsf, tots, gsb, seq("r"), pin, pin, pin,
      p["ssd_norm"], p["gla_norm"], p["pool_w"], p["pool_scale"], ef, eb, band)
    return ssd_out, pool_out, gla_out


def _sgu_stages(u_ref, v_ref, ng_ref, w_ref, bias_ref, out_ref, slot):
    gd = BRANCH_WIDTH // SGU_GROUPS
    for r in range(0, u_ref.shape[0], SGU_CHUNK):
        vf = _rms(_gelu_tanh(v_ref[r:r + SGU_CHUNK, :].astype(F32)), ng_ref[...]).astype(BF16)
        yield
        mixed = [_dot(w_ref[g], vf[:, g * gd:(g + 1) * gd]) for g in range(SGU_GROUPS)]
        uf = _gelu_tanh(u_ref[r:r + SGU_CHUNK, :].astype(F32))
        yield
        out_ref[slot, r:r + SGU_CHUNK, :] = (uf * (jnp.concatenate(mixed, axis=1) + bias_ref[...])).astype(BF16)
        yield


def _merge_ffn_stages(rows, x_ref, branches, gate_ref, wb_ref, wo_ref, gpost_ref, gpre_ref, w1_ref, w2_ref, gffn_ref,
                      o_ref):
    merged = None
    for n, load in branches:
        term = gate_ref[rows, n * D_MODEL:(n + 1) * D_MODEL].astype(F32) * _dot(load(rows), wb_ref[n])
        merged = term if merged is None else merged + term
        yield
    x1 = x_ref[rows, :] + _rms(_dot(merged.astype(BF16), wo_ref[...]), gpost_ref[...])
    h2 = _rms(x1, gpre_ref[...]).astype(BF16)
    yield
    ff = None
    for c in range(0, D_FF, D_MODEL):
        hid = jnp.square(jnp.maximum(_dot(h2, w1_ref[:, c:c + D_MODEL]), 0.0)).astype(BF16)
        yield
        part = _dot(hid, w2_ref[c:c + D_MODEL, :])
        ff = part if ff is None else ff + part
        yield
    o_ref[rows, :] = x1 + _rms(ff, gffn_ref[...])


def _merge_ffn_kernel(x_ref, b0_ref, b1_ref, u0_ref, v0_ref, un_ref, vn_ref, b3_ref, gate_ref, sng_ref, sw_ref,
                      sbias_ref, wb_ref, wo_ref, gpost_ref, gpre_ref, w1_ref, w2_ref, gffn_ref, o_ref, sgu_ref):
    i = pl.program_id(0)

    @pl.when(i == 0)
    def _():
        _run_staged([_sgu_stages(u0_ref, v0_ref, sng_ref, sw_ref, sbias_ref, sgu_ref, 0)])

    branches = ((0, lambda rows: b0_ref[rows, :]), (1, lambda rows: b1_ref[rows, :]),
                (3, lambda rows: b3_ref[rows, :]), (2, lambda rows: sgu_ref[i % 2, rows, :]))
    half = x_ref.shape[0] // 2
    _run_staged([
        _merge_ffn_stages(pl.ds(0, half), x_ref, branches, gate_ref, wb_ref, wo_ref, gpost_ref, gpre_ref, w1_ref,
                          w2_ref, gffn_ref, o_ref),
        _sgu_stages(un_ref, vn_ref, sng_ref, sw_ref, sbias_ref, sgu_ref, (i + 1) % 2),
        _merge_ffn_stages(pl.ds(half, half), x_ref, branches, gate_ref, wb_ref, wo_ref, gpost_ref, gpre_ref, w1_ref,
                          w2_ref, gffn_ref, o_ref)])


def _merge_ffn(x2d, ssd_out, pool_out, sgu_u, sgu_v, gla_out, gate, p, layer, tm):
    n = x2d.shape[0]
    steps = n // tm
    row = lambda w: pl.BlockSpec((tm, w), lambda i: (i, 0))
    first = pl.BlockSpec((tm, BRANCH_WIDTH), lambda i: (0, 0), pipeline_mode=pl.Buffered(1))
    ahead = pl.BlockSpec((tm, BRANCH_WIDTH), lambda i: (jnp.minimum(i + 1, steps - 1), 0))
    return pl.pallas_call(
        _merge_ffn_kernel,
        grid=(steps,),
        in_specs=[row(D_MODEL), row(BRANCH_WIDTH), row(BRANCH_WIDTH), first, first, ahead, ahead, row(BRANCH_WIDTH),
                  row(N_BRANCH * D_MODEL),
                  _wspec(p["sgu_norm"], layer), _wspec(p["sgu_w"], layer), _wspec(p["sgu_bias"], layer),
                  _wspec(p["w_branch"], layer), _wspec(p["w_out"], layer), _wspec(p["norm_mix_post"], layer),
                  _wspec(p["norm_ffn_pre"], layer), _wspec(p["w_ff1"], layer), _wspec(p["w_ff2"], layer),
                  _wspec(p["norm_ffn_post"], layer)],
        out_specs=row(D_MODEL),
        out_shape=jax.ShapeDtypeStruct((n, D_MODEL), F32),
        scratch_shapes=[pltpu.VMEM((2, tm, BRANCH_WIDTH), BF16)],
        compiler_params=_params(1),
        name="merge_ffn",
    )(x2d, ssd_out, pool_out, sgu_u, sgu_v, sgu_u, sgu_v, gla_out, gate, p["sgu_norm"], p["sgu_w"], p["sgu_bias"],
      p["w_branch"], p["w_out"], p["norm_mix_post"], p["norm_ffn_pre"], p["w_ff1"], p["w_ff2"],
      p["norm_ffn_post"])


def _pad_lanes(a, width):
    return jnp.pad(a, [(0, 0)] * (a.ndim - 1) + [(0, width - a.shape[-1])])


def _prepare_params(norm_mix_pre, w_in, ssd_conv_w, ssd_conv_b, ssd_a_log, ssd_dt_bias, ssd_d, ssd_norm, pool_w,
                    pool_scale, sgu_norm, sgu_w, sgu_b, gla_gate_w2, gla_gate_b, gla_norm, w_branch, w_out,
                    norm_mix_post, norm_ffn_pre, w_ff1, w_ff2, norm_ffn_post):
    depth = w_in.shape[0]
    pieces = []
    for _, sources, padded, _ in PROJ_GROUPS:
        cols = [w_in[:, :, _REF_COLS[s][0]:_REF_COLS[s][0] + _REF_COLS[s][1]] for s in sources]
        pieces.append(_pad_lanes(jnp.concatenate(cols, axis=-1), padded))
    row = lambda a: a.reshape(depth, 1, -1).astype(F32)
    lo = SSD_DT_LANES
    w2 = jnp.zeros((depth, LANE, 2 * GLA_QK), F32)
    w2 = w2.at[:, lo:lo + GLA_GATE_RANK, :GLA_QK].set(gla_gate_w2[:, 0])
    w2 = w2.at[:, lo + GLA_GATE_RANK:lo + 2 * GLA_GATE_RANK, GLA_QK:].set(gla_gate_w2[:, 1])
    return {
        "norm_mix_pre": row(norm_mix_pre),
        "w_in": jnp.concatenate(pieces, axis=-1).astype(BF16),
        "ssd_conv_w": jnp.pad(ssd_conv_w.astype(F32), [(0, 0), (0, 8 - SSD_CONV), (0, 0)]),
        "ssd_conv_b": row(ssd_conv_b),
        "ssd_a_log": _pad_lanes(row(ssd_a_log), LANE),
        "ssd_dt_bias": _pad_lanes(row(ssd_dt_bias), LANE),
        "ssd_d": row(jnp.repeat(ssd_d, SSD_HEAD_DIM, axis=-1)),
        "ssd_norm": row(ssd_norm),
        "pool_w": pool_w.astype(BF16),
        "pool_scale": row(pool_scale),
        "sgu_norm": row(sgu_norm),
        "sgu_w": sgu_w.astype(BF16),
        "sgu_bias": jnp.repeat(jnp.swapaxes(sgu_b, 1, 2), BRANCH_WIDTH // SGU_GROUPS, axis=-1).astype(F32),
        "gla_w2": w2.astype(BF16),
        "gla_gate_b": row(gla_gate_b),
        "gla_norm": row(gla_norm),
        "w_branch": w_branch.astype(BF16),
        "w_out": w_out.astype(BF16),
        "norm_mix_post": row(norm_mix_post),
        "norm_ffn_pre": row(norm_ffn_pre),
        "w_ff1": w_ff1.astype(BF16),
        "w_ff2": w_ff2.astype(BF16),
        "norm_ffn_post": row(norm_ffn_post),
    }


def _tiles(seq_len):
    return min(512, seq_len), min(512, seq_len), min(1024, seq_len)


def _trunk(x, p):
    b, l, d = x.shape
    tm_proj, tm_ffn, tl = _tiles(l)
    x2d = x.reshape(b * l, d)
    for layer in range(DEPTH):
        proj = dict(zip([g[0] for g in PROJ_GROUPS], _in_proj(x2d, p["norm_mix_pre"], p["w_in"], layer, tm_proj)))
        flat = lambda a: a.reshape(b * l, BRANCH_WIDTH)
        ssd_out, pool_out, gla_out = _mixers(proj, p, layer, b, l, tl)
        x2d = _merge_ffn(x2d, flat(ssd_out), flat(pool_out), proj["u"], proj["v"], flat(gla_out), proj["gate"],
                         p, layer, tm_ffn)
    return x2d.reshape(b, l, d)


def kernel(x_prompt, x_sample, norm_mix_pre, w_in, ssd_conv_w, ssd_conv_b, ssd_a_log, ssd_dt_bias, ssd_d, ssd_norm, pool_w, pool_scale, sgu_norm, sgu_w, sgu_b, gla_gate_w2, gla_gate_b, gla_norm, w_branch, w_out, norm_mix_post, norm_ffn_pre, w_ff1, w_ff2, norm_ffn_post):
    p = _prepare_params(norm_mix_pre, w_in, ssd_conv_w, ssd_conv_b, ssd_a_log, ssd_dt_bias, ssd_d, ssd_norm, pool_w,
                        pool_scale, sgu_norm, sgu_w, sgu_b, gla_gate_w2, gla_gate_b, gla_norm, w_branch, w_out,
                        norm_mix_post, norm_ffn_pre, w_ff1, w_ff2, norm_ffn_post)
    return (_trunk(x_prompt, p), _trunk(x_sample, p))
```
